```python
import math
import jax, jax.numpy as jnp
from jax import lax
import numpy as np

D_MODEL = 4096
BATCH = 4
SEQ = 4096
DEPTH = 2

BRANCH_WIDTH = D_MODEL // 4
HEAD_DIM = 128
SB_HEADS = BRANCH_WIDTH // HEAD_DIM
DSA_HEADS = BRANCH_WIDTH // HEAD_DIM
CONV_CH = BRANCH_WIDTH
CONV_WIDTH = 31
IDX_HEADS = 16
IDX_DIM = 64
TOPK_MAX = 256
N_BRANCH = 3
D_FF = 14336
FFN_CONV_WIDTH = 3
ROPE_THETA = 500000.0
ROT_DIM = HEAD_DIM // 4
IDX_ROT_DIM = IDX_DIM // 4
Q_BLOCK = 128
N_MOD = 6
EPS = 1e-6
IN_WIDTH = (3 * BRANCH_WIDTH + 2 * CONV_CH + 3 * BRANCH_WIDTH
            + IDX_HEADS * IDX_DIM + IDX_DIM + IDX_HEADS + N_BRANCH * D_MODEL)

kernel_name = "hybrid_sb_conformer_dsa_gated_block"


def rmsnorm(x, g):
    xf = x.astype(jnp.float32)
    y = xf * lax.rsqrt(jnp.mean(xf * xf, axis=-1, keepdims=True) + EPS)
    return (y * g.astype(jnp.float32)).astype(x.dtype)


def layernorm(x, g, b):
    xf = x.astype(jnp.float32)
    mu = jnp.mean(xf, axis=-1, keepdims=True)
    var = jnp.mean(jnp.square(xf - mu), axis=-1, keepdims=True)
    y = (xf - mu) * lax.rsqrt(var + EPS)
    return (y * g.astype(jnp.float32) + b.astype(jnp.float32)).astype(x.dtype)


def rope_tables(positions, rot_dim):
    inv_freq = 1.0 / jnp.power(ROPE_THETA, jnp.arange(0, rot_dim, 2, dtype=jnp.float32) / rot_dim)
    ang = positions.astype(jnp.float32)[..., None] * inv_freq
    return jnp.cos(ang), jnp.sin(ang)


def apply_partial_rope(x, cos, sin):
    r = 2 * cos.shape[-1]
    xr = x[..., :r].astype(jnp.float32)
    x1, x2 = xr[..., : r // 2], xr[..., r // 2:]
    rot = jnp.concatenate([x1 * cos - x2 * sin, x2 * cos + x1 * sin], axis=-1).astype(x.dtype)
    return jnp.concatenate([rot, x[..., r:]], axis=-1)


def causal_depthwise_conv(x, w):
    width, ch = w.shape
    return lax.conv_general_dilated(
        x, w.astype(x.dtype)[:, None, :], window_strides=(1,), padding=[(width - 1, 0)],
        dimension_numbers=("NWC", "WIO", "NWC"), feature_group_count=ch)


def to_blocks(a):
    b, s = a.shape[0], a.shape[1]
    return jnp.moveaxis(a.reshape(b, s // Q_BLOCK, Q_BLOCK, *a.shape[2:]), 1, 0)


def from_blocks(a):
    a = jnp.moveaxis(a, 0, 1)
    return a.reshape(a.shape[0], a.shape[1] * a.shape[2], *a.shape[3:])


def stick_breaking_attention(q, k, v):
    s_len, d = q.shape[1], q.shape[-1]
    scale = d ** -0.5
    kpos = jnp.arange(s_len)

    def block(args):
        qb, start = args
        z = jnp.einsum("bqhd,bshd->bhqs", qb, k).astype(jnp.float32) * scale
        tpos = start + jnp.arange(Q_BLOCK)
        mask = kpos[None, :] < tpos[:, None]
        log1m = jnp.where(mask, jax.nn.log_sigmoid(-z), 0.0)
        excl = lax.cumsum(log1m, axis=3, reverse=True) - log1m
        a = jnp.where(mask, jnp.exp(jax.nn.log_sigmoid(z) + excl), 0.0)
        return jnp.einsum("bhqs,bshd->bqhd", a.astype(v.dtype), v)

    starts = jnp.arange(s_len // Q_BLOCK, dtype=jnp.int32) * Q_BLOCK
    return from_blocks(lax.map(block, (to_blocks(q), starts)))


def dsa_attention(q, k, v, qi, ki, wi, k_top):
    s_len, d = q.shape[1], q.shape[-1]
    scale = d ** -0.5
    idx_scale = IDX_DIM ** -0.5
    kpos = jnp.arange(s_len)
    gather = jax.vmap(lambda tb, ib: tb[ib])

    def block(args):
        qb, qib, wib, start = args
        dots = jnp.einsum("bqhe,bse->bqhs", qib, ki).astype(jnp.float32) * idx_scale
        score = jnp.einsum("bqhs,bqh->bqs", jax.nn.relu(dots), wib.astype(jnp.float32))
        tpos = start + jnp.arange(Q_BLOCK)
        admissible = kpos[None, :] <= tpos[:, None]
        score = jnp.where(admissible[None], score, -jnp.inf)
        _, idx = lax.top_k(score, k_top)
        valid = idx <= tpos[None, :, None]
        kg = gather(k, idx)
        vg = gather(v, idx)
        logits = jnp.einsum("bqhd,bqnhd->bqhn", qb, kg).astype(jnp.float32) * scale
        logits = jnp.where(valid[:, :, None, :], logits, -jnp.inf)
        p = jax.nn.softmax(logits, axis=-1)
        return jnp.einsum("bqhn,bqnhd->bqhd", p.astype(v.dtype), vg)

    starts = jnp.arange(s_len // Q_BLOCK, dtype=jnp.int32) * Q_BLOCK
    out = lax.map(block, (to_blocks(q), to_blocks(qi), to_blocks(wi), starts))
    return from_blocks(out)


def conformer_conv(glu_in, conv_w, conv_b, ln_g, ln_b):
    a, g = jnp.split(glu_in, 2, axis=-1)
    u = a * jax.nn.sigmoid(g)
    u = causal_depthwise_conv(u, conv_w) + conv_b
    return jax.nn.silu(layernorm(u, ln_g, ln_b))


def hybrid_mixer(h, w_in, conv_w, conv_b, ln_g, ln_b, w_branch, w_out,
                 cos_a, sin_a, cos_i, sin_i, k_top):
    b, s, _ = h.shape
    proj = h @ w_in
    sizes = (BRANCH_WIDTH, BRANCH_WIDTH, BRANCH_WIDTH, 2 * CONV_CH,
             BRANCH_WIDTH, BRANCH_WIDTH, BRANCH_WIDTH,
             IDX_HEADS * IDX_DIM, IDX_DIM, IDX_HEADS, N_BRANCH * D_MODEL)
    cuts, acc = [], 0
    for n in sizes[:-1]:
        acc += n
        cuts.append(acc)
    qa, ka, va, glu, qc, kc, vc, qi, ki, wi, gl = jnp.split(proj, cuts, axis=-1)

    out_a = stick_breaking_attention(qa.reshape(b, s, SB_HEADS, HEAD_DIM),
                                     ka.reshape(b, s, SB_HEADS, HEAD_DIM),
                                     va.reshape(b, s, SB_HEADS, HEAD_DIM)).reshape(b, s, BRANCH_WIDTH)
    out_b = conformer_conv(glu, conv_w, conv_b, ln_g, ln_b)
    ca, sa = cos_a[:, :, None, :], sin_a[:, :, None, :]
    qc = apply_partial_rope(qc.reshape(b, s, DSA_HEADS, HEAD_DIM), ca, sa)
    kc = apply_partial_rope(kc.reshape(b, s, DSA_HEADS, HEAD_DIM), ca, sa)
    vc = vc.reshape(b, s, DSA_HEADS, HEAD_DIM)
    qi = apply_partial_rope(qi.reshape(b, s, IDX_HEADS, IDX_DIM), cos_i[:, :, None, :], sin_i[:, :, None, :])
    ki = apply_partial_rope(ki, cos_i, sin_i)
    wi = wi * (IDX_HEADS ** -0.5)
    out_c = dsa_attention(qc, kc, vc, qi, ki, wi, k_top).reshape(b, s, BRANCH_WIDTH)

    gates = jax.nn.sigmoid(gl.reshape(b, s, N_BRANCH, D_MODEL))
    merged = (gates[:, :, 0] * (out_a @ w_branch[0])
              + gates[:, :, 1] * (out_b @ w_branch[1])
              + gates[:, :, 2] * (out_c @ w_branch[2]))
    return merged @ w_out


def conv_gated_mlp(h, w_up, conv_w, w_down):
    u = causal_depthwise_conv(h @ w_up, conv_w)
    g, v = jnp.split(u, 2, axis=-1)
    return (jax.nn.silu(g) * v) @ w_down


def setup_inputs(seed: int = 0) -> dict:
    key = jax.random.key(seed)
    ks = jax.random.split(key, 24)
    f32 = jnp.float32
    nrm = lambda k, shape, s: jax.random.normal(k, shape, f32) * s
    L, D = DEPTH, D_MODEL
    x = nrm(ks[0], (BATCH, SEQ, D), 1.0)
    c = nrm(ks[1], (BATCH, D), 1.0)
    offsets = jax.random.randint(ks[2], (BATCH, 1), 0, SEQ, dtype=jnp.int32)
    positions = (offsets + jnp.arange(SEQ, dtype=jnp.int32)[None, :]).astype(jnp.int32)
    return {
        "x": x,
        "c": c,
        "positions": positions,
        "ada_w": nrm(ks[3], (L, D, N_MOD * D), 0.5 * D ** -0.5),
        "ada_b": nrm(ks[4], (L, N_MOD * D), 0.02),
        "norm_pre_mix": 1.0 + nrm(ks[5], (L, D), 0.05),
        "norm_post_mix": 1.0 + nrm(ks[6], (L, D), 0.05),
        "norm_pre_ffn": 1.0 + nrm(ks[7], (L, D), 0.05),
        "norm_post_ffn": 1.0 + nrm(ks[8], (L, D), 0.05),
        "w_in": nrm(ks[9], (L, D, IN_WIDTH), D ** -0.5),
        "conv_w": nrm(ks[10], (L, CONV_WIDTH, CONV_CH), CONV_WIDTH ** -0.5),
        "conv_b": nrm(ks[11], (L, CONV_CH), 0.02),
        "conv_ln_g": 1.0 + nrm(ks[12], (L, CONV_CH), 0.05),
        "conv_ln_b": nrm(ks[13], (L, CONV_CH), 0.02),
        "w_branch": nrm(ks[14], (L, N_BRANCH, BRANCH_WIDTH, D), BRANCH_WIDTH ** -0.5),
        "w_out": nrm(ks[15], (L, D, D), D ** -0.5),
        "w_up": nrm(ks[16], (L, D, 2 * D_FF), D ** -0.5),
        "ffn_conv_w": nrm(ks[17], (L, FFN_CONV_WIDTH, 2 * D_FF), FFN_CONV_WIDTH ** -0.5),
        "w_down": nrm(ks[18], (L, D_FF, D), D_FF ** -0.5),
    }


def reference(x, c, positions, ada_w, ada_b, norm_pre_mix, norm_post_mix, norm_pre_ffn,
              norm_post_ffn, w_in, conv_w, conv_b, conv_ln_g, conv_ln_b, w_branch, w_out,
              w_up, ffn_conv_w, w_down):
    s_len = x.shape[1]
    k_top = min(TOPK_MAX, s_len // 4)
    cos_a, sin_a = rope_tables(positions, ROT_DIM)
    cos_i, sin_i = rope_tables(positions, IDX_ROT_DIM)
    cond = jax.nn.silu(c)
    for l in range(DEPTH):
        mod = cond @ ada_w[l] + ada_b[l]
        sh_m, sc_m, g_m, sh_f, sc_f, g_f = [m[:, None, :] for m in jnp.split(mod, N_MOD, axis=-1)]
        h = rmsnorm(x, norm_pre_mix[l]) * (1.0 + sc_m) + sh_m
        y = hybrid_mixer(h, w_in[l], conv_w[l], conv_b[l], conv_ln_g[l], conv_ln_b[l],
                         w_branch[l], w_out[l], cos_a, sin_a, cos_i, sin_i, k_top)
        x = x + g_m * rmsnorm(y, norm_post_mix[l])
        h = rmsnorm(x, norm_pre_ffn[l]) * (1.0 + sc_f) + sh_f
        y = conv_gated_mlp(h, w_up[l], ffn_conv_w[l], w_down[l])
        x = x + g_f * rmsnorm(y, norm_post_ffn[l])
    return x
```

```python
import functools

import jax
import jax.numpy as jnp
from jax import lax
from jax.experimental import pallas as pl
from jax.experimental.pallas import tpu as pltpu

HEAD_DIM = 128
IDX_HEADS = 16
IDX_DIM = 64
TOPK_MAX = 256
ROPE_THETA = 500000.0
ROT_DIM = HEAD_DIM // 4
IDX_ROT_DIM = IDX_DIM // 4
N_MOD = 6
EPS = 1e-6

V7X_LANES = 128
V7X_VMEM_BYTES = 64 * 1024 * 1024
V7X_VMEM_LIMIT = V7X_VMEM_BYTES - 6 * 1024 * 1024

CONV_HALO = 32
MASKED = -1e30

F32 = jnp.float32
BF16 = jnp.bfloat16
I32 = jnp.int32
INT_MIN = -2 ** 31


def _cparams(*sem):
    return pltpu.CompilerParams(dimension_semantics=sem, vmem_limit_bytes=V7X_VMEM_LIMIT)


def _tile(dim, pref):
    t = min(dim, pref)
    while dim % t:
        t //= 2
    return t


def _iota(shape, axis):
    return lax.broadcasted_iota(I32, shape, axis)


def _rms(x):
    return x * lax.rsqrt(jnp.mean(x * x, axis=-1, keepdims=True) + EPS)


def _dot(a, b):
    return jnp.dot(a, b, preferred_element_type=F32)


def _dot_nt(a, b):
    return lax.dot_general(a, b, (((1,), (1,)), ((), ())), preferred_element_type=F32)


def _mod_kernel(c_ref, w_ref, b_ref, o_ref):
    c = c_ref[...]
    cond = c * jax.nn.sigmoid(c)
    o_ref[...] = _dot(cond.astype(BF16), w_ref[...].astype(BF16)) + b_ref[...]


def _modulation(c8, ada_w, ada_b):
    L, D, M = ada_w.shape
    tn = _tile(M, 1024)
    return pl.pallas_call(
        _mod_kernel,
        out_shape=jax.ShapeDtypeStruct((L, 8, M), F32),
        grid=(L, M // tn),
        in_specs=[pl.BlockSpec((8, D), lambda l, j: (0, 0)),
                  pl.BlockSpec((None, D, tn), lambda l, j: (l, 0, j)),
                  pl.BlockSpec((None, 1, tn), lambda l, j: (l, 0, j))],
        out_specs=pl.BlockSpec((None, 8, tn), lambda l, j: (l, 0, j)),
        compiler_params=_cparams("arbitrary", "arbitrary"),
        name="adaln_mod",
    )(c8, ada_w, ada_b.reshape(L, 1, M))


def _rope_tab_kernel(pos_ref, f_ref, s_ref, ca_ref, sa_ref, ci_ref, si_ref):
    p = pos_ref[...].astype(F32)
    ang = p * f_ref[0:1, :]
    ca_ref[...] = jnp.cos(ang)
    sa_ref[...] = jnp.sin(ang) * s_ref[0:1, :]
    ang = p * f_ref[1:2, :]
    ci_ref[...] = jnp.cos(ang)
    si_ref[...] = jnp.sin(ang) * s_ref[1:2, :]


def _rope_tables(positions):
    n = positions.size
    lane = jnp.arange(V7X_LANES)

    def rows(rot, period):
        inv = 1.0 / jnp.power(ROPE_THETA, jnp.arange(0, rot, 2, dtype=F32) / rot)
        lp = lane % period
        half = rot // 2
        f = jnp.where(lp < rot, inv[lp % half], 0.0)
        s = jnp.where(lp < half, -1.0, jnp.where(lp < rot, 1.0, 0.0))
        return f, s

    fa, sa = rows(ROT_DIM, HEAD_DIM)
    fi, si = rows(IDX_ROT_DIM, IDX_DIM)
    f = jnp.zeros((8, V7X_LANES), F32).at[0].set(fa).at[1].set(fi)
    s = jnp.zeros((8, V7X_LANES), F32).at[0].set(sa).at[1].set(si)
    posb = jnp.broadcast_to(positions.reshape(n, 1), (n, V7X_LANES))
    tm = _tile(n, 1024)
    spec = pl.BlockSpec((tm, V7X_LANES), lambda i: (i, 0))
    cst = pl.BlockSpec((8, V7X_LANES), lambda i: (0, 0))
    out = jax.ShapeDtypeStruct((n, V7X_LANES), F32)
    return pl.pallas_call(
        _rope_tab_kernel, out_shape=(out,) * 4, grid=(n // tm,),
        in_specs=[spec, cst, cst], out_specs=(spec,) * 4,
        compiler_params=_cparams("arbitrary"), name="rope_tables",
    )(posb, f, s)


def _rope(x, cos, sin, half, period):
    lane = _iota(x.shape, 1) % period
    partner = jnp.where(lane < half,
                        pltpu.roll(x, V7X_LANES - half, 1),
                        pltpu.roll(x, half, 1))
    return x * cos + partner * sin


def _prenorm_kernel(x_ref, g_ref, sc_ref, sh_ref, h_ref):
    y = _rms(x_ref[...]) * g_ref[...]
    h_ref[...] = (y * (1.0 + sc_ref[...]) + sh_ref[...]).astype(h_ref.dtype)


def _mod_spec(d, which, tpb):
    return pl.BlockSpec((None, None, 1, d), lambda i: (which, i // tpb, 0, 0))


def _prenorm(x, g, mod_l, which_sc, which_sh, seq):
    n, d = x.shape
    tm = _tile(seq, 256)
    tpb = seq // tm
    return pl.pallas_call(
        _prenorm_kernel,
        out_shape=jax.ShapeDtypeStruct((n, d), BF16),
        grid=(n // tm,),
        in_specs=[pl.BlockSpec((tm, d), lambda i: (i, 0)),
                  pl.BlockSpec((1, d), lambda i: (0, 0)),
                  _mod_spec(d, which_sc, tpb), _mod_spec(d, which_sh, tpb)],
        out_specs=pl.BlockSpec((tm, d), lambda i: (i, 0)),
        compiler_params=_cparams("arbitrary"), name="prenorm",
    )(x, g.reshape(1, d), mod_l, mod_l)


def _post_kernel(y_ref, x_ref, gp_ref, gate_ref, *rest, with_next):
    xn = x_ref[...] + gate_ref[...] * (_rms(y_ref[...]) * gp_ref[...])
    if with_next:
        gn_ref, sc_ref, sh_ref, xo_ref, h_ref = rest
        xo_ref[...] = xn
        h_ref[...] = (_rms(xn) * gn_ref[...] * (1.0 + sc_ref[...]) + sh_ref[...]).astype(h_ref.dtype)
    else:
        (xo_ref,) = rest
        xo_ref[...] = xn


def _post(y, x, g_post, mod_l, which_gate, seq, nxt=None):
    n, d = x.shape
    tm = _tile(seq, 256)
    tpb = seq // tm
    row = pl.BlockSpec((tm, d), lambda i: (i, 0))
    vec = pl.BlockSpec((1, d), lambda i: (0, 0))
    ins = [y, x, g_post.reshape(1, d), mod_l]
    specs = [row, row, vec, _mod_spec(d, which_gate, tpb)]
    outs = [jax.ShapeDtypeStruct((n, d), F32)]
    ospecs = [row]
    if nxt is not None:
        g_next, mod_n, which_sc, which_sh = nxt
        ins += [g_next.reshape(1, d), mod_n, mod_n]
        specs += [vec, _mod_spec(d, which_sc, tpb), _mod_spec(d, which_sh, tpb)]
        outs.append(jax.ShapeDtypeStruct((n, d), BF16))
        ospecs.append(row)
    res = pl.pallas_call(
        functools.partial(_post_kernel, with_next=nxt is not None),
        out_shape=tuple(outs), grid=(n // tm,), in_specs=specs, out_specs=tuple(ospecs),
        compiler_params=_cparams("arbitrary"), name="post_norm_residual",
    )(*ins)
    return res if nxt is not None else (res[0], None)


def _mm_kernel(a_ref, w_ref, o_ref):
    o_ref[...] = _dot(a_ref[...], w_ref[...]).astype(o_ref.dtype)


def _matmul(a, w, out_dtype, name):
    m, k = a.shape
    n = w.shape[1]
    tm, tn = _tile(m, 1024), _tile(n, 1024)
    return pl.pallas_call(
        _mm_kernel,
        out_shape=jax.ShapeDtypeStruct((m, n), out_dtype),
        grid=(m // tm, n // tn),
        in_specs=[pl.BlockSpec((tm, k), lambda i, j: (i, 0)),
                  pl.BlockSpec((k, tn), lambda i, j: (0, j))],
        out_specs=pl.BlockSpec((tm, tn), lambda i, j: (i, j)),
        compiler_params=_cparams("arbitrary", "arbitrary"), name=name,
    )(a, w)


def _mm_acc_kernel(a_ref, w_ref, o_ref):
    @pl.when(pl.program_id(2) == 0)
    def _():
        o_ref[...] = jnp.zeros(o_ref.shape, F32)

    o_ref[...] += _dot(a_ref[...], w_ref[...])


def _matmul_kacc(a, w, name):
    m, k = a.shape
    n = w.shape[1]
    tm, tn, tk = _tile(m, 1024), _tile(n, 2048), _tile(k, 1024)
    return pl.pallas_call(
        _mm_acc_kernel,
        out_shape=jax.ShapeDtypeStruct((m, n), F32),
        grid=(m // tm, n // tn, k // tk),
        in_specs=[pl.BlockSpec((tm, tk), lambda i, j, kk: (i, kk)),
                  pl.BlockSpec((tk, tn), lambda i, j, kk: (kk, j))],
        out_specs=pl.BlockSpec((tm, tn), lambda i, j, kk: (i, j)),
        compiler_params=_cparams("arbitrary", "arbitrary", "arbitrary"), name=name,
    )(a, w)


def _kiwi_kernel(a_ref, w_ref, ci_ref, si_ref, o_ref, ob_ref):
    x = _dot(a_ref[...], w_ref[...])
    lane = _iota(x.shape, 1)
    is_k = lane < IDX_DIM
    c = jnp.where(is_k, ci_ref[...], 1.0)
    s = jnp.where(is_k, si_ref[...], 0.0)
    y = _rope(x, c, s, IDX_ROT_DIM // 2, IDX_DIM)
    wscale = (IDX_HEADS ** -0.5) * (IDX_DIM ** -0.5)
    y = y * jnp.where(is_k, 1.0, wscale)
    o_ref[...] = y
    ob_ref[...] = y.astype(BF16)


def _kiwi(h, w_small, ci, si):
    n, k = h.shape
    tm = _tile(n, 1024)
    row = pl.BlockSpec((tm, V7X_LANES), lambda i: (i, 0))
    return pl.pallas_call(
        _kiwi_kernel,
        out_shape=(jax.ShapeDtypeStruct((n, V7X_LANES), F32), jax.ShapeDtypeStruct((n, V7X_LANES), BF16)),
        grid=(n // tm,),
        in_specs=[pl.BlockSpec((tm, k), lambda i: (i, 0)),
                  pl.BlockSpec((k, V7X_LANES), lambda i: (0, 0)), row, row],
        out_specs=(row, row),
        compiler_params=_cparams("arbitrary"), name="inproj_indexer_kw",
    )(h, w_small, ci, si)


def _prep_kernel(va_ref, qc_ref, kc_ref, vc_ref, qi_ref, ca_ref, sa_ref, ci_ref, si_ref,
                 vat_ref, vct_ref, qcr_ref, kcr_ref, qiz_ref):
    heads = va_ref.shape[1] // HEAD_DIM
    ca, sa, ci, si = ca_ref[...], sa_ref[...], ci_ref[...], si_ref[...]
    for h in range(heads):
        sl = slice(h * HEAD_DIM, (h + 1) * HEAD_DIM)
        vat_ref[h] = va_ref[:, sl].astype(F32).T.astype(BF16)
        vct_ref[h] = vc_ref[:, sl].astype(F32).T.astype(BF16)
        qcr_ref[:, sl] = _rope(qc_ref[:, sl].astype(F32), ca, sa, ROT_DIM // 2, HEAD_DIM).astype(BF16)
        kcr_ref[:, sl] = _rope(kc_ref[:, sl].astype(F32), ca, sa, ROT_DIM // 2, HEAD_DIM).astype(BF16)
    lane = _iota(ci.shape, 1)
    for t in range(IDX_HEADS * IDX_DIM // V7X_LANES):
        sl = slice(t * V7X_LANES, (t + 1) * V7X_LANES)
        y = _rope(qi_ref[:, sl].astype(F32), ci, si, IDX_ROT_DIM // 2, IDX_DIM)
        even = jnp.where(lane < IDX_DIM, y, 0.0)
        odd = jnp.where(lane < IDX_DIM, pltpu.roll(y, IDX_DIM, 1), 0.0)
        qiz_ref[:, 2 * t * V7X_LANES:(2 * t + 1) * V7X_LANES] = even.astype(BF16)
        qiz_ref[:, (2 * t + 1) * V7X_LANES:(2 * t + 2) * V7X_LANES] = odd.astype(BF16)


def _prep(proj, tabs, lay, batch, seq):
    n = proj.shape[0]
    bw, heads = lay["bw"], lay["bw"] // HEAD_DIM
    qw = IDX_HEADS * IDX_DIM
    tm = _tile(seq, 512)
    tpb = seq // tm

    def cols(off, width):
        return pl.BlockSpec((tm, width), lambda i: (i, off // width))

    tab = pl.BlockSpec((tm, V7X_LANES), lambda i: (i, 0))
    vt = pl.BlockSpec((None, heads, HEAD_DIM, tm), lambda i: (i // tpb, 0, 0, i % tpb))
    rowb = pl.BlockSpec((tm, bw), lambda i: (i, 0))
    ca, sa, ci, si = tabs
    return pl.pallas_call(
        _prep_kernel,
        out_shape=(jax.ShapeDtypeStruct((batch, heads, HEAD_DIM, seq), BF16),
                   jax.ShapeDtypeStruct((batch, heads, HEAD_DIM, seq), BF16),
                   jax.ShapeDtypeStruct((n, bw), BF16),
                   jax.ShapeDtypeStruct((n, bw), BF16),
                   jax.ShapeDtypeStruct((n, 2 * qw), BF16)),
        grid=(n // tm,),
        in_specs=[cols(lay["va"], bw), cols(lay["qc"], bw), cols(lay["kc"], bw), cols(lay["vc"], bw),
                  cols(lay["qi"], qw), tab, tab, tab, tab],
        out_specs=(vt, vt, rowb, rowb, pl.BlockSpec((tm, 2 * qw), lambda i: (i, 0))),
        compiler_params=_cparams("arbitrary"), name="attn_prep",
    )(proj, proj, proj, proj, proj, ca, sa, ci, si)


def _sb_kernel(q_ref, k_ref, vt_ref, o_ref, *, tq, scale):
    i = pl.program_id(2)
    q = q_ref[...]
    row, col = _iota((tq, tq), 0), _iota((tq, tq), 1)
    after = jnp.where(col > row, 1.0, 0.0).astype(BF16)
    causal = row < col

    def tile(s0, carry, diag):
        z = _dot_nt(k_ref[pl.ds(s0, tq), :], q) * scale
        ls = jnp.minimum(z, 0.0) - jnp.log(1.0 + jnp.exp(-jnp.abs(z)))
        l1m = ls - z
        if diag:
            l1m = jnp.where(causal, l1m, 0.0)
        hi = l1m.astype(BF16)
        lo = (l1m - hi.astype(F32)).astype(BF16)
        excl = _dot(after, hi) + _dot(after, lo)
        a = jnp.exp(ls + excl + carry)
        if diag:
            a = jnp.where(causal, a, 0.0)
        pv = _dot(vt_ref[:, pl.ds(s0, tq)], a.astype(BF16))
        return pv, carry + jnp.sum(l1m, axis=0, keepdims=True)

    acc, carry = tile(pl.multiple_of(i * tq, tq), jnp.zeros((1, tq), F32), True)

    def body(j, st):
        pv, c = tile(pl.multiple_of((i - 1 - j) * tq, tq), st[1], False)
        return st[0] + pv, c

    acc, carry = lax.fori_loop(0, i, body, (acc, carry))
    o_ref[...] = acc.T.astype(o_ref.dtype)


def _sb_attention(proj, vat, lay, batch, seq):
    n = proj.shape[0]
    bw, heads = lay["bw"], lay["bw"] // HEAD_DIM
    tq = _tile(seq, 256)
    nq = seq // tq
    qb, kb = lay["qa"] // HEAD_DIM, lay["ka"] // HEAD_DIM
    return pl.pallas_call(
        functools.partial(_sb_kernel, tq=tq, scale=HEAD_DIM ** -0.5),
        out_shape=jax.ShapeDtypeStruct((n, bw), BF16),
        grid=(batch, heads, nq),
        in_specs=[pl.BlockSpec((tq, HEAD_DIM), lambda b, h, i: (b * nq + i, qb + h)),
                  pl.BlockSpec((seq, HEAD_DIM), lambda b, h, i: (b, kb + h)),
                  pl.BlockSpec((None, None, HEAD_DIM, seq), lambda b, h, i: (b, h, 0, 0))],
        out_specs=pl.BlockSpec((tq, HEAD_DIM), lambda b, h, i: (b * nq + i, h)),
        compiler_params=_cparams("arbitrary", "arbitrary", "arbitrary"), name="sb_attention",
    )(proj, proj, vat)


def _conformer_kernel(a_ref, g_ref, ha_ref, hg_ref, cw_ref, cb_ref, lg_ref, lb_ref, o_ref,
                      u_ref, y_ref, *, tm, tpb, rc, cc):
    width, ch = cw_ref.shape
    first = (pl.program_id(0) % tpb) == 0
    hu = ha_ref[...].astype(F32) * jax.nn.sigmoid(hg_ref[...].astype(F32))
    u_ref[0:CONV_HALO, :] = jnp.where(first, 0.0, hu)
    u_ref[CONV_HALO:, :] = a_ref[...].astype(F32) * jax.nn.sigmoid(g_ref[...].astype(F32))
    lead = CONV_HALO - (width - 1)

    def rows(r, _):
        r0 = pl.multiple_of(r * rc, rc)
        for c in range(ch // cc):
            cs = slice(c * cc, (c + 1) * cc)
            win = u_ref[pl.ds(r0, rc + CONV_HALO), cs]
            acc = jnp.zeros((rc, cc), F32)
            for k in range(width):
                acc = acc + cw_ref[k:k + 1, cs] * win[lead + k:lead + k + rc, :]
            y_ref[:, cs] = acc + cb_ref[:, cs]
        y = y_ref[...]
        mu = jnp.mean(y, axis=-1, keepdims=True)
        yc = y - mu
        var = jnp.mean(yc * yc, axis=-1, keepdims=True)
        yn = yc * lax.rsqrt(var + EPS) * lg_ref[...] + lb_ref[...]
        o_ref[pl.ds(r0, rc), :] = (yn * jax.nn.sigmoid(yn)).astype(o_ref.dtype)
        return 0

    lax.fori_loop(0, tm // rc, rows, 0)


def _conformer(proj, conv_w, conv_b, ln_g, ln_b, lay, seq):
    n = proj.shape[0]
    width, ch = conv_w.shape
    assert width - 1 <= CONV_HALO
    tm = _tile(seq, 512)
    tpb = seq // tm
    rc, cc = _tile(tm, 64), _tile(ch, 256)
    a_blk, hb = lay["glu"] // ch, tm // CONV_HALO
    vec = pl.BlockSpec((1, ch), lambda i: (0, 0))
    return pl.pallas_call(
        functools.partial(_conformer_kernel, tm=tm, tpb=tpb, rc=rc, cc=cc),
        out_shape=jax.ShapeDtypeStruct((n, ch), BF16),
        grid=(n // tm,),
        in_specs=[pl.BlockSpec((tm, ch), lambda i: (i, a_blk)),
                  pl.BlockSpec((tm, ch), lambda i: (i, a_blk + 1)),
                  pl.BlockSpec((CONV_HALO, ch), lambda i: (jnp.maximum(i * hb - 1, 0), a_blk)),
                  pl.BlockSpec((CONV_HALO, ch), lambda i: (jnp.maximum(i * hb - 1, 0), a_blk + 1)),
                  pl.BlockSpec((width, ch), lambda i: (0, 0)), vec, vec, vec],
        out_specs=pl.BlockSpec((tm, ch), lambda i: (i, 0)),
        scratch_shapes=[pltpu.VMEM((tm + CONV_HALO, ch), F32), pltpu.VMEM((rc, ch), F32)],
        compiler_params=_cparams("arbitrary"), name="conformer_conv",
    )(proj, proj, proj, proj, conv_w, conv_b.reshape(1, ch), ln_g.reshape(1, ch), ln_b.reshape(1, ch))


def _dsa_kernel(qc_ref, kc_ref, vt_ref, qiz_ref, kb_ref, kwq_ref, o_ref, key_ref, *, tq, k_top, scale):
    i = pl.program_id(1)
    heads = qc_ref.shape[1] // HEAD_DIM
    seq_bits = max(1, (kc_ref.shape[0] - 1).bit_length())
    row, col = _iota((tq, tq), 0), _iota((tq, tq), 1)
    admissible = row <= col
    wi_t = kwq_ref[...].T

    def start(c):
        return pl.multiple_of(c * tq, tq)

    def score_tile(c, diag):
        kb = kb_ref[pl.ds(start(c), tq), :]
        acc = jnp.zeros((tq, tq), F32)
        for h in range(IDX_HEADS):
            d = _dot_nt(kb, qiz_ref[:, h * V7X_LANES:(h + 1) * V7X_LANES])
            acc = acc + jnp.maximum(d, 0.0) * wi_t[IDX_DIM + h:IDX_DIM + h + 1, :]
        if diag:
            acc = jnp.where(admissible, acc, -jnp.inf)
        bits = lax.bitcast_convert_type(acc, I32)
        key_ref[pl.ds(start(c), tq), :] = bits ^ (lax.shift_right_arithmetic(bits, 31) & 0x7FFFFFFF)

    def score_body(c, _):
        score_tile(c, False)
        return 0

    lax.fori_loop(0, i, score_body, 0)
    score_tile(i, True)

    def count(pred):
        def body(c, cnt):
            hit = pred(key_ref[pl.ds(start(c), tq), :], c)
            return cnt + jnp.sum(jnp.where(hit, 1.0, 0.0), axis=0, keepdims=True)
        return lax.fori_loop(0, i + 1, body, jnp.zeros((1, tq), F32))

    def bit_body(b, ans):
        cand = ans | lax.shift_left(jnp.int32(1), 31 - b)
        cnt = count(lambda k, c: k >= (cand ^ INT_MIN))
        return jnp.where(cnt >= k_top, cand, ans)

    thr = lax.fori_loop(0, 32, bit_body, jnp.zeros((1, tq), I32)) ^ INT_MIN

    n_ge = count(lambda k, c: k >= thr)

    @pl.when(jnp.max(n_ge) > k_top)
    def _():
        need = k_top - count(lambda k, c: k > thr)

        def pos_body(b, x):
            cand = x | lax.shift_left(jnp.int32(1), seq_bits - 1 - b)
            below = count(lambda k, c: (k == thr) & (row + c * tq < cand))
            return jnp.where(below < need, cand, x)

        last = lax.fori_loop(0, seq_bits, pos_body, jnp.zeros((1, tq), I32))

        def demote(c, _):
            k = key_ref[pl.ds(start(c), tq), :]
            drop = (k == thr) & (row + c * tq > last) & (n_ge > k_top)
            key_ref[pl.ds(start(c), tq), :] = jnp.where(drop, thr - 1, k)
            return 0

        lax.fori_loop(0, i + 1, demote, 0)

    for h in range(heads):
        hs = slice(h * HEAD_DIM, (h + 1) * HEAD_DIM)
        q = qc_ref[:, hs]

        def attend(c, st, diag):
            m, l, acc = st
            s0 = start(c)
            lg = _dot_nt(kc_ref[pl.ds(s0, tq), hs], q) * scale
            sel = key_ref[pl.ds(s0, tq), :] >= thr
            if diag:
                sel = sel & admissible
            lgm = jnp.where(sel, lg, MASKED)
            m_new = jnp.maximum(m, jnp.max(lgm, axis=0, keepdims=True))
            alpha = jnp.exp(m - m_new)
            p = jnp.where(sel, jnp.exp(lgm - m_new), 0.0)
            l = alpha * l + jnp.sum(p, axis=0, keepdims=True)
            acc = alpha * acc + _dot(vt_ref[h, :, pl.ds(s0, tq)], p.astype(BF16))
            return m_new, l, acc

        st = (jnp.full((1, tq), MASKED, F32), jnp.zeros((1, tq), F32), jnp.zeros((HEAD_DIM, tq), F32))
        st = lax.fori_loop(0, i, lambda c, s: attend(c, s, False), st)
        _, l, acc = attend(i, st, True)
        o_ref[:, hs] = (acc / l).T.astype(o_ref.dtype)


def _dsa_attention(qcr, kcr, vct, qiz, kwb, kw, batch, seq, k_top):
    n, bw = qcr.shape
    heads = bw // HEAD_DIM
    tq = _tile(seq, 256)
    nq = seq // tq
    return pl.pallas_call(
        functools.partial(_dsa_kernel, tq=tq, k_top=k_top, scale=HEAD_DIM ** -0.5),
        out_shape=jax.ShapeDtypeStruct((n, bw), BF16),
        grid=(batch, nq),
        in_specs=[pl.BlockSpec((tq, bw), lambda b, i: (b * nq + i, 0)),
                  pl.BlockSpec((seq, bw), lambda b, i: (b, 0)),
                  pl.BlockSpec((None, heads, HEAD_DIM, seq), lambda b, i: (b, 0, 0, 0)),
                  pl.BlockSpec((tq, qiz.shape[1]), lambda b, i: (b * nq + i, 0)),
                  pl.BlockSpec((seq, V7X_LANES), lambda b, i: (b, 0)),
                  pl.BlockSpec((tq, V7X_LANES), lambda b, i: (b * nq + i, 0))],
        out_specs=pl.BlockSpec((tq, bw), lambda b, i: (b * nq + i, 0)),
        scratch_shapes=[pltpu.VMEM((seq, tq), I32)],
        compiler_params=_cparams("arbitrary", "arbitrary"), name="dsa_attention",
    )(qcr, kcr, vct, qiz, kwb, kw)


def _merge_kernel(a_ref, b_ref, c_ref, w_ref, ga_ref, gb_ref, gc_ref, o_ref):
    acc = jax.nn.sigmoid(ga_ref[...].astype(F32)) * _dot(a_ref[...], w_ref[0])
    acc = acc + jax.nn.sigmoid(gb_ref[...].astype(F32)) * _dot(b_ref[...], w_ref[1])
    acc = acc + jax.nn.sigmoid(gc_ref[...].astype(F32)) * _dot(c_ref[...], w_ref[2])
    o_ref[...] = acc.astype(o_ref.dtype)


def _merge(out_a, out_b, out_c, w_branch, proj, lay):
    n, bw = out_a.shape
    d = w_branch.shape[2]
    tm, tn = _tile(n, 1024), _tile(d, 1024)
    assert lay["gl"] % tn == 0
    g0, gstep = lay["gl"] // tn, d // tn
    br = pl.BlockSpec((tm, bw), lambda i, j: (i, 0))

    def gate(g):
        return pl.BlockSpec((tm, tn), lambda i, j: (i, g0 + g * gstep + j))

    return pl.pallas_call(
        _merge_kernel,
        out_shape=jax.ShapeDtypeStruct((n, d), BF16),
        grid=(n // tm, d // tn),
        in_specs=[br, br, br, pl.BlockSpec((3, bw, tn), lambda i, j: (0, 0, j)), gate(0), gate(1), gate(2)],
        out_specs=pl.BlockSpec((tm, tn), lambda i, j: (i, j)),
        compiler_params=_cparams("arbitrary", "arbitrary"), name="branch_merge",
    )(out_a, out_b, out_c, w_branch, proj, proj, proj)


def _ffn_up_kernel(a_ref, wg_ref, wv_ref, cg_ref, cv_ref, o_ref, carry_ref, *, tm, tpb):
    i, j = pl.program_id(0), pl.program_id(1)
    width = cg_ref.shape[0]
    a = a_ref[...]

    @pl.when(i % tpb == 0)
    def _():
        carry_ref[j] = jnp.zeros(carry_ref.shape[1:], F32)

    def conv(w_ref, c_ref, slot):
        p = _dot(a, w_ref[...])
        ext = jnp.concatenate([carry_ref[j, slot], p], axis=0)
        carry_ref[j, slot] = p[tm - 8:, :]
        out = c_ref[width - 1:width, :] * p
        for k in range(width - 1):
            sh = width - 1 - k
            out = out + c_ref[k:k + 1, :] * ext[8 - sh:8 - sh + tm, :]
        return out

    g = conv(wg_ref, cg_ref, 0)
    v = conv(wv_ref, cv_ref, 1)
    o_ref[...] = (g * jax.nn.sigmoid(g) * v).astype(o_ref.dtype)


def _ffn_up(h, w_up, conv_w, seq):
    n, d = h.shape
    width, f2 = conv_w.shape
    f = f2 // 2
    assert width - 1 <= 8
    tm, tn = _tile(seq, 1024), _tile(f, 512)
    tpb, nj = seq // tm, f // tn
    return pl.pallas_call(
        functools.partial(_ffn_up_kernel, tm=tm, tpb=tpb),
        out_shape=jax.ShapeDtypeStruct((n, f), BF16),
        grid=(n // tm, nj),
        in_specs=[pl.BlockSpec((tm, d), lambda i, j: (i, 0)),
                  pl.BlockSpec((d, tn), lambda i, j: (0, j)),
                  pl.BlockSpec((d, tn), lambda i, j: (0, nj + j)),
                  pl.BlockSpec((width, tn), lambda i, j: (0, j)),
                  pl.BlockSpec((width, tn), lambda i, j: (0, nj + j))],
        out_specs=pl.BlockSpec((tm, tn), lambda i, j: (i, j)),
        scratch_shapes=[pltpu.VMEM((nj, 2, 8, tn), F32)],
        compiler_params=_cparams("arbitrary", "arbitrary"), name="ffn_up_conv_gate",
    )(h, w_up, w_up, conv_w, conv_w)


def _layout(bw, ch):
    lay = {"bw": bw, "qa": 0, "ka": bw, "va": 2 * bw, "glu": 3 * bw}
    lay["qc"] = 3 * bw + 2 * ch
    lay["kc"] = lay["qc"] + bw
    lay["vc"] = lay["kc"] + bw
    lay["qi"] = lay["vc"] + bw
    lay["gl"] = lay["qi"] + IDX_HEADS * IDX_DIM
    return lay


def kernel(x, c, positions, ada_w, ada_b, norm_pre_mix, norm_post_mix, norm_pre_ffn, norm_post_ffn,
           w_in, conv_w, conv_b, conv_ln_g, conv_ln_b, w_branch, w_out, w_up, ffn_conv_w, w_down):
    batch, seq, d = x.shape
    depth = ada_w.shape[0]
    bw, ch = w_branch.shape[2], conv_w.shape[2]
    assert bw == ch and bw % HEAD_DIM == 0 and batch <= 8
    n = batch * seq
    k_top = min(TOPK_MAX, seq // 4)
    lay = _layout(bw, ch)
    small0 = lay["gl"]
    small_w = IDX_DIM + IDX_HEADS

    tabs = _rope_tables(positions)
    c8 = jnp.zeros((8, d), F32).at[:batch].set(c)
    mod = _modulation(c8, ada_w, ada_b)
    mod = mod[:, :batch].reshape(depth, batch, N_MOD, d).transpose(0, 2, 1, 3)
    mod = mod.reshape(depth, N_MOD, batch, 1, d)
    SH_M, SC_M, G_M, SH_F, SC_F, G_F = range(N_MOD)

    xf = x.reshape(n, d)
    h = _prenorm(xf, norm_pre_mix[0], mod[0], SC_M, SH_M, seq)
    for l in range(depth):
        w_main = jnp.concatenate([w_in[l, :, :small0], w_in[l, :, small0 + small_w:]], axis=1).astype(BF16)
        w_small = jnp.pad(w_in[l, :, small0:small0 + small_w], ((0, 0), (0, V7X_LANES - small_w))).astype(BF16)

        proj = _matmul(h, w_main, BF16, "inproj_main")
        kw, kwb = _kiwi(h, w_small, tabs[2], tabs[3])
        vat, vct, qcr, kcr, qiz = _prep(proj, tabs, lay, batch, seq)
        out_a = _sb_attention(proj, vat, lay, batch, seq)
        out_b = _conformer(proj, conv_w[l], conv_b[l], conv_ln_g[l], conv_ln_b[l], lay, seq)
        out_c = _dsa_attention(qcr, kcr, vct, qiz, kwb, kw, batch, seq, k_top)
        merged = _merge(out_a, out_b, out_c, w_branch[l].astype(BF16), proj, lay)
        y = _matmul(merged, w_out[l].astype(BF16), F32, "outproj")
        xf, h = _post(y, xf, norm_post_mix[l], mod[l], G_M, seq,
                      nxt=(norm_pre_ffn[l], mod[l], SC_F, SH_F))

        act = _ffn_up(h, w_up[l].astype(BF16), ffn_conv_w[l], seq)
        y = _matmul_kacc(act, w_down[l].astype(BF16), "ffn_down")
        nxt = (norm_pre_mix[l + 1], mod[l + 1], SC_M, SH_M) if l + 1 < depth else None
        xf, h = _post(y, xf, norm_post_ffn[l], mod[l], G_F, seq, nxt=nxt)
    return xf.reshape(batch, seq, d)
```

```python
import functools

import jax
import jax.numpy as jnp
from jax import lax
from jax.experimental import pallas as pl
from jax.experimental.pallas import tpu as pltpu

HEAD_DIM = 128
IDX_HEADS = 16
IDX_DIM = 64
TOPK_MAX = 256
ROPE_THETA = 500000.0
ROT_DIM = HEAD_DIM // 4
IDX_ROT_DIM = IDX_DIM // 4
N_MOD = 6
EPS = 1e-6

V7X_LANES = 128
V7X_VMEM_BYTES = 64 * 1024 * 1024
V7X_VMEM_LIMIT = V7X_VMEM_BYTES - 6 * 1024 * 1024

CONV_HALO = 32
MASKED = -1e30
SB_DEAD = -120.0

F32 = jnp.float32
BF16 = jnp.bfloat16
I32 = jnp.int32
INT_MIN = -2 ** 31


def _cparams(*sem):
    return pltpu.CompilerParams(dimension_semantics=sem, vmem_limit_bytes=V7X_VMEM_LIMIT)


def _tile(dim, pref):
    t = min(dim, pref)
    while dim % t:
        t //= 2
    return t


def _iota(shape, axis):
    return lax.broadcasted_iota(I32, shape, axis)


def _rms(x):
    return x * lax.rsqrt(jnp.mean(x * x, axis=-1, keepdims=True) + EPS)


def _dot(a, b):
    return jnp.dot(a, b, preferred_element_type=F32)


def _dot_nt(a, b):
    return lax.dot_general(a, b, (((1,), (1,)), ((), ())), preferred_element_type=F32)


def _mod_kernel(c_ref, w_ref, b_ref, o_ref):
    c = c_ref[...]
    cond = c * jax.nn.sigmoid(c)
    o_ref[...] = _dot(cond.astype(BF16), w_ref[...].astype(BF16)) + b_ref[...]


def _modulation(c8, ada_w, ada_b):
    L, D, M = ada_w.shape
    tn = _tile(M, 1024)
    return pl.pallas_call(
        _mod_kernel,
        out_shape=jax.ShapeDtypeStruct((L, 8, M), F32),
        grid=(L, M // tn),
        in_specs=[pl.BlockSpec((8, D), lambda l, j: (0, 0)),
                  pl.BlockSpec((None, D, tn), lambda l, j: (l, 0, j)),
                  pl.BlockSpec((None, 1, tn), lambda l, j: (l, 0, j))],
        out_specs=pl.BlockSpec((None, 8, tn), lambda l, j: (l, 0, j)),
        compiler_params=_cparams("arbitrary", "arbitrary"),
        name="adaln_mod",
    )(c8, ada_w, ada_b.reshape(L, 1, M))


def _rope_tab_kernel(pos_ref, f_ref, s_ref, ca_ref, sa_ref, ci_ref, si_ref):
    p = pos_ref[...].astype(F32)
    ang = p * f_ref[0:1, :]
    ca_ref[...] = jnp.cos(ang)
    sa_ref[...] = jnp.sin(ang) * s_ref[0:1, :]
    ang = p * f_ref[1:2, :]
    ci_ref[...] = jnp.cos(ang)
    si_ref[...] = jnp.sin(ang) * s_ref[1:2, :]


def _rope_tables(positions):
    n = positions.size
    lane = jnp.arange(V7X_LANES)

    def rows(rot, period):
        inv = 1.0 / jnp.power(ROPE_THETA, jnp.arange(0, rot, 2, dtype=F32) / rot)
        lp = lane % period
        half = rot // 2
        f = jnp.where(lp < rot, inv[lp % half], 0.0)
        s = jnp.where(lp < half, -1.0, jnp.where(lp < rot, 1.0, 0.0))
        return f, s

    fa, sa = rows(ROT_DIM, HEAD_DIM)
    fi, si = rows(IDX_ROT_DIM, IDX_DIM)
    f = jnp.zeros((8, V7X_LANES), F32).at[0].set(fa).at[1].set(fi)
    s = jnp.zeros((8, V7X_LANES), F32).at[0].set(sa).at[1].set(si)
    posb = jnp.broadcast_to(positions.reshape(n, 1), (n, V7X_LANES))
    tm = _tile(n, 1024)
    spec = pl.BlockSpec((tm, V7X_LANES), lambda i: (i, 0))
    cst = pl.BlockSpec((8, V7X_LANES), lambda i: (0, 0))
    out = jax.ShapeDtypeStruct((n, V7X_LANES), F32)
    return pl.pallas_call(
        _rope_tab_kernel, out_shape=(out,) * 4, grid=(n // tm,),
        in_specs=[spec, cst, cst], out_specs=(spec,) * 4,
        compiler_params=_cparams("arbitrary"), name="rope_tables",
    )(posb, f, s)


def _rope(x, cos, sin, half, period):
    lane = _iota(x.shape, 1) % period
    partner = jnp.where(lane < half,
                        pltpu.roll(x, V7X_LANES - half, 1),
                        pltpu.roll(x, half, 1))
    return x * cos + partner * sin


def _prenorm_kernel(x_ref, g_ref, sc_ref, sh_ref, h_ref):
    y = _rms(x_ref[...]) * g_ref[...]
    h_ref[...] = (y * (1.0 + sc_ref[...]) + sh_ref[...]).astype(h_ref.dtype)


def _mod_spec(d, which, tpb):
    return pl.BlockSpec((None, None, 1, d), lambda i: (which, i // tpb, 0, 0))


def _prenorm(x, g, mod_l, which_sc, which_sh, seq):
    n, d = x.shape
    tm = _tile(seq, 256)
    tpb = seq // tm
    return pl.pallas_call(
        _prenorm_kernel,
        out_shape=jax.ShapeDtypeStruct((n, d), BF16),
        grid=(n // tm,),
        in_specs=[pl.BlockSpec((tm, d), lambda i: (i, 0)),
                  pl.BlockSpec((1, d), lambda i: (0, 0)),
                  _mod_spec(d, which_sc, tpb), _mod_spec(d, which_sh, tpb)],
        out_specs=pl.BlockSpec((tm, d), lambda i: (i, 0)),
        compiler_params=_cparams("arbitrary"), name="prenorm",
    )(x, g.reshape(1, d), mod_l, mod_l)


def _post_kernel(y_ref, x_ref, gp_ref, gate_ref, *rest, with_next):
    xn = x_ref[...] + gate_ref[...] * (_rms(y_ref[...]) * gp_ref[...])
    if with_next:
        gn_ref, sc_ref, sh_ref, xo_ref, h_ref = rest
        xo_ref[...] = xn
        h_ref[...] = (_rms(xn) * gn_ref[...] * (1.0 + sc_ref[...]) + sh_ref[...]).astype(h_ref.dtype)
    else:
        (xo_ref,) = rest
        xo_ref[...] = xn


def _post(y, x, g_post, mod_l, which_gate, seq, nxt=None):
    n, d = x.shape
    tm = _tile(seq, 256)
    tpb = seq // tm
    row = pl.BlockSpec((tm, d), lambda i: (i, 0))
    vec = pl.BlockSpec((1, d), lambda i: (0, 0))
    ins = [y, x, g_post.reshape(1, d), mod_l]
    specs = [row, row, vec, _mod_spec(d, which_gate, tpb)]
    outs = [jax.ShapeDtypeStruct((n, d), F32)]
    ospecs = [row]
    if nxt is not None:
        g_next, mod_n, which_sc, which_sh = nxt
        ins += [g_next.reshape(1, d), mod_n, mod_n]
        specs += [vec, _mod_spec(d, which_sc, tpb), _mod_spec(d, which_sh, tpb)]
        outs.append(jax.ShapeDtypeStruct((n, d), BF16))
        ospecs.append(row)
    res = pl.pallas_call(
        functools.partial(_post_kernel, with_next=nxt is not None),
        out_shape=tuple(outs), grid=(n // tm,), in_specs=specs, out_specs=tuple(ospecs),
        compiler_params=_cparams("arbitrary"), name="post_norm_residual",
    )(*ins)
    return res if nxt is not None else (res[0], None)


def _mm_kernel(a_ref, w_ref, o_ref):
    o_ref[...] = _dot(a_ref[...], w_ref[...].astype(BF16)).astype(o_ref.dtype)


def _resident_rows(tm, k):
    return pl.BlockSpec((tm, k), lambda i, j: (i, 0), pipeline_mode=pl.Buffered(1))


def _matmul(a, w, layer, n, out_dtype, name):
    m, k = a.shape
    tm, tn = _tile(m, 2048), _tile(n, 512)
    if layer is None:
        w_spec = pl.BlockSpec((k, tn), lambda i, j: (0, j))
    else:
        w_spec = pl.BlockSpec((None, k, tn), lambda i, j: (layer, 0, j))
    return pl.pallas_call(
        _mm_kernel,
        out_shape=jax.ShapeDtypeStruct((m, n), out_dtype),
        grid=(m // tm, n // tn),
        in_specs=[_resident_rows(tm, k), w_spec],
        out_specs=pl.BlockSpec((tm, tn), lambda i, j: (i, j)),
        compiler_params=_cparams("arbitrary", "arbitrary"), name=name,
    )(a, w)


def _mm_acc_kernel(a_ref, w_ref, o_ref):
    @pl.when(pl.program_id(2) == 0)
    def _():
        o_ref[...] = jnp.zeros(o_ref.shape, F32)

    o_ref[...] += _dot(a_ref[...], w_ref[...])


def _matmul_kacc(a, w, name):
    m, k = a.shape
    n = w.shape[1]
    tm, tn, tk = _tile(m, 1024), _tile(n, 2048), _tile(k, 2048)
    return pl.pallas_call(
        _mm_acc_kernel,
        out_shape=jax.ShapeDtypeStruct((m, n), F32),
        grid=(m // tm, n // tn, k // tk),
        in_specs=[pl.BlockSpec((tm, tk), lambda i, j, kk: (i, kk)),
                  pl.BlockSpec((tk, tn), lambda i, j, kk: (kk, j))],
        out_specs=pl.BlockSpec((tm, tn), lambda i, j, kk: (i, j)),
        compiler_params=_cparams("arbitrary", "arbitrary", "arbitrary"), name=name,
    )(a, w)


def _kiwi_kernel(a_ref, w_ref, ci_ref, si_ref, o_ref, ob_ref):
    x = _dot(a_ref[...], w_ref[...])
    lane = _iota(x.shape, 1)
    is_k = lane < IDX_DIM
    c = jnp.where(is_k, ci_ref[...], 1.0)
    s = jnp.where(is_k, si_ref[...], 0.0)
    y = _rope(x, c, s, IDX_ROT_DIM // 2, IDX_DIM)
    wscale = (IDX_HEADS ** -0.5) * (IDX_DIM ** -0.5)
    y = y * jnp.where(is_k, 1.0, wscale)
    o_ref[...] = y
    ob_ref[...] = y.astype(BF16)


def _kiwi(h, w_small, ci, si):
    n, k = h.shape
    tm = _tile(n, 1024)
    row = pl.BlockSpec((tm, V7X_LANES), lambda i: (i, 0))
    return pl.pallas_call(
        _kiwi_kernel,
        out_shape=(jax.ShapeDtypeStruct((n, V7X_LANES), F32), jax.ShapeDtypeStruct((n, V7X_LANES), BF16)),
        grid=(n // tm,),
        in_specs=[pl.BlockSpec((tm, k), lambda i: (i, 0)),
                  pl.BlockSpec((k, V7X_LANES), lambda i: (0, 0)), row, row],
        out_specs=(row, row),
        compiler_params=_cparams("arbitrary"), name="inproj_indexer_kw",
    )(h, w_small, ci, si)


def _prep_kernel(va_ref, qc_ref, kc_ref, vc_ref, qi_ref, ca_ref, sa_ref, ci_ref, si_ref,
                 vat_ref, vct_ref, qcr_ref, kcr_ref, qiz_ref):
    heads = va_ref.shape[1] // HEAD_DIM
    ca, sa, ci, si = ca_ref[...], sa_ref[...], ci_ref[...], si_ref[...]
    for h in range(heads):
        sl = slice(h * HEAD_DIM, (h + 1) * HEAD_DIM)
        vat_ref[h] = va_ref[:, sl].astype(F32).T.astype(BF16)
        vct_ref[h] = vc_ref[:, sl].astype(F32).T.astype(BF16)
        qcr_ref[:, sl] = _rope(qc_ref[:, sl].astype(F32), ca, sa, ROT_DIM // 2, HEAD_DIM).astype(BF16)
        kcr_ref[:, sl] = _rope(kc_ref[:, sl].astype(F32), ca, sa, ROT_DIM // 2, HEAD_DIM).astype(BF16)
    lane = _iota(ci.shape, 1)
    for t in range(IDX_HEADS * IDX_DIM // V7X_LANES):
        sl = slice(t * V7X_LANES, (t + 1) * V7X_LANES)
        y = _rope(qi_ref[:, sl].astype(F32), ci, si, IDX_ROT_DIM // 2, IDX_DIM)
        even = jnp.where(lane < IDX_DIM, y, 0.0)
        odd = jnp.where(lane < IDX_DIM, pltpu.roll(y, IDX_DIM, 1), 0.0)
        qiz_ref[:, 2 * t * V7X_LANES:(2 * t + 1) * V7X_LANES] = even.astype(BF16)
        qiz_ref[:, (2 * t + 1) * V7X_LANES:(2 * t + 2) * V7X_LANES] = odd.astype(BF16)


def _prep(proj, tabs, lay, batch, seq):
    n = proj.shape[0]
    bw, heads = lay["bw"], lay["bw"] // HEAD_DIM
    qw = IDX_HEADS * IDX_DIM
    tm = _tile(seq, 512)
    tpb = seq // tm

    def cols(off, width):
        return pl.BlockSpec((tm, width), lambda i: (i, off // width))

    tab = pl.BlockSpec((tm, V7X_LANES), lambda i: (i, 0))
    vt = pl.BlockSpec((None, heads, HEAD_DIM, tm), lambda i: (i // tpb, 0, 0, i % tpb))
    rowb = pl.BlockSpec((tm, bw), lambda i: (i, 0))
    ca, sa, ci, si = tabs
    return pl.pallas_call(
        _prep_kernel,
        out_shape=(jax.ShapeDtypeStruct((batch, heads, HEAD_DIM, seq), BF16),
                   jax.ShapeDtypeStruct((batch, heads, HEAD_DIM, seq), BF16),
                   jax.ShapeDtypeStruct((n, bw), BF16),
                   jax.ShapeDtypeStruct((n, bw), BF16),
                   jax.ShapeDtypeStruct((n, 2 * qw), BF16)),
        grid=(n // tm,),
        in_specs=[cols(lay["va"], bw), cols(lay["qc"], bw), cols(lay["kc"], bw), cols(lay["vc"], bw),
                  cols(lay["qi"], qw), tab, tab, tab, tab],
        out_specs=(vt, vt, rowb, rowb, pl.BlockSpec((tm, 2 * qw), lambda i: (i, 0))),
        compiler_params=_cparams("arbitrary"), name="attn_prep",
    )(proj, proj, proj, proj, proj, ca, sa, ci, si)


def _sb_kernel(q_ref, k_ref, vt_ref, o_ref, acc_ref, *, tq, scale):
    i = pl.program_id(1)
    heads = q_ref.shape[1] // HEAD_DIM
    row, col = _iota((tq, tq), 0), _iota((tq, tq), 1)
    after = jnp.where(col > row, 1.0, 0.0).astype(BF16)
    causal = row < col

    def all_heads(s0, carry, diag):
        hr = range(heads)
        hs = [slice(h * HEAD_DIM, (h + 1) * HEAD_DIM) for h in hr]
        zs = [_dot_nt(k_ref[pl.ds(s0, tq), hs[h]], q_ref[:, hs[h]]) for h in hr]
        lss, l1ms, his, los = [], [], [], []
        for h in hr:
            z = zs[h] * scale
            ls = jnp.minimum(z, 0.0) - jnp.log(1.0 + jnp.exp(-jnp.abs(z)))
            l1m = ls - z
            if diag:
                l1m = jnp.where(causal, l1m, 0.0)
            hi = l1m.astype(BF16)
            lss.append(ls)
            l1ms.append(l1m)
            his.append(hi)
            los.append((l1m - hi.astype(F32)).astype(BF16))
        excls = [_dot(after, his[h]) + _dot(after, los[h]) for h in hr]
        ps = []
        for h in hr:
            a = jnp.exp(lss[h] + excls[h] + carry[h:h + 1])
            if diag:
                a = jnp.where(causal, a, 0.0)
            ps.append(a.astype(BF16))
        pvs = [_dot(vt_ref[h, :, pl.ds(s0, tq)], ps[h]) for h in hr]
        for h in hr:
            acc_ref[h] = pvs[h] if diag else acc_ref[h] + pvs[h]
        return carry + jnp.concatenate([jnp.sum(l1ms[h], axis=0, keepdims=True) for h in hr], axis=0)

    def alive(carry):
        return (jnp.max(carry) > SB_DEAD).astype(I32)

    carry = all_heads(pl.multiple_of(i * tq, tq), jnp.zeros((heads, tq), F32), True)

    def body(st):
        j, _, carry = st
        carry = all_heads(pl.multiple_of((i - 1 - j) * tq, tq), carry, False)
        return j + 1, alive(carry), carry

    lax.while_loop(lambda st: (st[0] < i) & (st[1] > 0), body, (jnp.int32(0), alive(carry), carry))
    for h in range(heads):
        o_ref[:, h * HEAD_DIM:(h + 1) * HEAD_DIM] = acc_ref[h].T.astype(o_ref.dtype)


def _sb_attention(proj, vat, lay, batch, seq):
    n = proj.shape[0]
    bw, heads = lay["bw"], lay["bw"] // HEAD_DIM
    tq = _tile(seq, 256)
    nq = seq // tq
    qb, kb = lay["qa"] // bw, lay["ka"] // bw
    return pl.pallas_call(
        functools.partial(_sb_kernel, tq=tq, scale=HEAD_DIM ** -0.5),
        out_shape=jax.ShapeDtypeStruct((n, bw), BF16),
        grid=(batch, nq),
        in_specs=[pl.BlockSpec((tq, bw), lambda b, i: (b * nq + i, qb)),
                  pl.BlockSpec((seq, bw), lambda b, i: (b, kb)),
                  pl.BlockSpec((None, heads, HEAD_DIM, seq), lambda b, i: (b, 0, 0, 0))],
        out_specs=pl.BlockSpec((tq, bw), lambda b, i: (b * nq + i, 0)),
        scratch_shapes=[pltpu.VMEM((heads, HEAD_DIM, tq), F32)],
        compiler_params=_cparams("arbitrary", "arbitrary"), name="sb_attention",
    )(proj, proj, vat)


def _conformer_kernel(a_ref, g_ref, ha_ref, hg_ref, cw_ref, cb_ref, lg_ref, lb_ref, o_ref,
                      u_ref, y_ref, *, tm, tpb, rc, cc):
    width, ch = cw_ref.shape
    first = (pl.program_id(0) % tpb) == 0
    hu = ha_ref[...].astype(F32) * jax.nn.sigmoid(hg_ref[...].astype(F32))
    u_ref[0:CONV_HALO, :] = jnp.where(first, 0.0, hu)
    u_ref[CONV_HALO:, :] = a_ref[...].astype(F32) * jax.nn.sigmoid(g_ref[...].astype(F32))
    lead = CONV_HALO - (width - 1)

    def rows(r, _):
        r0 = pl.multiple_of(r * rc, rc)
        for c in range(ch // cc):
            cs = slice(c * cc, (c + 1) * cc)
            win = u_ref[pl.ds(r0, rc + CONV_HALO), cs]
            acc = jnp.zeros((rc, cc), F32)
            for k in range(width):
                acc = acc + cw_ref[k:k + 1, cs] * win[lead + k:lead + k + rc, :]
            y_ref[:, cs] = acc + cb_ref[:, cs]
        y = y_ref[...]
        mu = jnp.mean(y, axis=-1, keepdims=True)
        yc = y - mu
        var = jnp.mean(yc * yc, axis=-1, keepdims=True)
        yn = yc * lax.rsqrt(var + EPS) * lg_ref[...] + lb_ref[...]
        o_ref[pl.ds(r0, rc), :] = (yn * jax.nn.sigmoid(yn)).astype(o_ref.dtype)
        return 0

    lax.fori_loop(0, tm // rc, rows, 0)


def _conformer(proj, conv_w, conv_b, ln_g, ln_b, lay, seq):
    n = proj.shape[0]
    width, ch = conv_w.shape
    assert width - 1 <= CONV_HALO
    tm = _tile(seq, 512)
    tpb = seq // tm
    rc, cc = _tile(tm, 64), _tile(ch, 256)
    a_blk, hb = lay["glu"] // ch, tm // CONV_HALO
    vec = pl.BlockSpec((1, ch), lambda i: (0, 0))
    return pl.pallas_call(
        functools.partial(_conformer_kernel, tm=tm, tpb=tpb, rc=rc, cc=cc),
        out_shape=jax.ShapeDtypeStruct((n, ch), BF16),
        grid=(n // tm,),
        in_specs=[pl.BlockSpec((tm, ch), lambda i: (i, a_blk)),
                  pl.BlockSpec((tm, ch), lambda i: (i, a_blk + 1)),
                  pl.BlockSpec((CONV_HALO, ch), lambda i: (jnp.maximum(i * hb - 1, 0), a_blk)),
                  pl.BlockSpec((CONV_HALO, ch), lambda i: (jnp.maximum(i * hb - 1, 0), a_blk + 1)),
                  pl.BlockSpec((width, ch), lambda i: (0, 0)), vec, vec, vec],
        out_specs=pl.BlockSpec((tm, ch), lambda i: (i, 0)),
        scratch_shapes=[pltpu.VMEM((tm + CONV_HALO, ch), F32), pltpu.VMEM((rc, ch), F32)],
        compiler_params=_cparams("arbitrary"), name="conformer_conv",
    )(proj, proj, proj, proj, conv_w, conv_b.reshape(1, ch), ln_g.reshape(1, ch), ln_b.reshape(1, ch))


def _dsa_kernel(qc_ref, kc_ref, vt_ref, qiz_ref, kb_ref, kwq_ref, o_ref, key_ref, acc_ref, *, tq, k_top, scale):
    i = pl.program_id(1)
    heads = qc_ref.shape[1] // HEAD_DIM
    seq_bits = max(1, (kc_ref.shape[0] - 1).bit_length())
    row, col = _iota((tq, tq), 0), _iota((tq, tq), 1)
    admissible = row <= col
    wi_t = kwq_ref[...].T

    def start(c):
        return pl.multiple_of(c * tq, tq)

    def score_tile(c, diag):
        kb = kb_ref[pl.ds(start(c), tq), :]
        acc = jnp.zeros((tq, tq), F32)
        for h in range(IDX_HEADS):
            d = _dot_nt(kb, qiz_ref[:, h * V7X_LANES:(h + 1) * V7X_LANES])
            acc = acc + jnp.maximum(d, 0.0) * wi_t[IDX_DIM + h:IDX_DIM + h + 1, :]
        if diag:
            acc = jnp.where(admissible, acc, -jnp.inf)
        bits = lax.bitcast_convert_type(acc, I32)
        key_ref[pl.ds(start(c), tq), :] = bits ^ (lax.shift_right_arithmetic(bits, 31) & 0x7FFFFFFF)

    def score_body(c, _):
        score_tile(c, False)
        return 0

    lax.fori_loop(0, i, score_body, 0)
    score_tile(i, True)

    def count(pred):
        def body(c, cnt):
            hit = pred(key_ref[pl.ds(start(c), tq), :], c)
            ones = jnp.where(hit, 1.0, 0.0).reshape(tq // 8, 8, tq)
            return cnt + jnp.sum(ones, axis=0)
        part = lax.fori_loop(0, i + 1, body, jnp.zeros((8, tq), F32))
        return jnp.sum(part, axis=0, keepdims=True)

    def bit_body(b, ans):
        cand = ans | lax.shift_left(jnp.int32(1), 31 - b)
        cnt = count(lambda k, c: k >= (cand ^ INT_MIN))
        return jnp.where(cnt >= k_top, cand, ans)

    thr = lax.fori_loop(0, 32, bit_body, jnp.zeros((1, tq), I32)) ^ INT_MIN

    n_ge = count(lambda k, c: k >= thr)

    @pl.when(jnp.max(n_ge) > k_top)
    def _():
        need = k_top - count(lambda k, c: k > thr)

        def pos_body(b, x):
            cand = x | lax.shift_left(jnp.int32(1), seq_bits - 1 - b)
            below = count(lambda k, c: (k == thr) & (row + c * tq < cand))
            return jnp.where(below < need, cand, x)

        last = lax.fori_loop(0, seq_bits, pos_body, jnp.zeros((1, tq), I32))

        def demote(c, _):
            k = key_ref[pl.ds(start(c), tq), :]
            drop = (k == thr) & (row + c * tq > last) & (n_ge > k_top)
            key_ref[pl.ds(start(c), tq), :] = jnp.where(drop, thr - 1, k)
            return 0

        lax.fori_loop(0, i + 1, demote, 0)

    def attend(c, st, diag):
        m, l = st
        s0 = start(c)
        sel = key_ref[pl.ds(s0, tq), :] >= thr
        if diag:
            sel = sel & admissible
        hr = range(heads)
        hs = [slice(h * HEAD_DIM, (h + 1) * HEAD_DIM) for h in hr]
        lgs = [_dot_nt(kc_ref[pl.ds(s0, tq), hs[h]], qc_ref[:, hs[h]]) for h in hr]
        ms, ls, alphas, ps = [], [], [], []
        for h in hr:
            lgm = jnp.where(sel, lgs[h] * scale, MASKED)
            m_new = jnp.maximum(m[h:h + 1], jnp.max(lgm, axis=0, keepdims=True))
            alpha = jnp.exp(m[h:h + 1] - m_new)
            p = jnp.where(sel, jnp.exp(lgm - m_new), 0.0)
            ls.append(alpha * l[h:h + 1] + jnp.sum(p, axis=0, keepdims=True))
            ms.append(m_new)
            alphas.append(alpha)
            ps.append(p.astype(BF16))
        pvs = [_dot(vt_ref[h, :, pl.ds(s0, tq)], ps[h]) for h in hr]
        for h in hr:
            acc_ref[h] = alphas[h] * acc_ref[h] + pvs[h]
        return jnp.concatenate(ms, axis=0), jnp.concatenate(ls, axis=0)

    acc_ref[...] = jnp.zeros(acc_ref.shape, F32)
    st = (jnp.full((heads, tq), MASKED, F32), jnp.zeros((heads, tq), F32))
    st = lax.fori_loop(0, i, lambda c, s: attend(c, s, False), st)
    _, l = attend(i, st, True)
    for h in range(heads):
        o_ref[:, h * HEAD_DIM:(h + 1) * HEAD_DIM] = (acc_ref[h] / l[h:h + 1]).T.astype(o_ref.dtype)


def _dsa_attention(qcr, kcr, vct, qiz, kwb, kw, batch, seq, k_top):
    n, bw = qcr.shape
    heads = bw // HEAD_DIM
    tq = _tile(seq, 256)
    nq = seq // tq
    return pl.pallas_call(
        functools.partial(_dsa_kernel, tq=tq, k_top=k_top, scale=HEAD_DIM ** -0.5),
        out_shape=jax.ShapeDtypeStruct((n, bw), BF16),
        grid=(batch, nq),
        in_specs=[pl.BlockSpec((tq, bw), lambda b, i: (b * nq + i, 0)),
                  pl.BlockSpec((seq, bw), lambda b, i: (b, 0)),
                  pl.BlockSpec((None, heads, HEAD_DIM, seq), lambda b, i: (b, 0, 0, 0)),
                  pl.BlockSpec((tq, qiz.shape[1]), lambda b, i: (b * nq + i, 0)),
                  pl.BlockSpec((seq, V7X_LANES), lambda b, i: (b, 0)),
                  pl.BlockSpec((tq, V7X_LANES), lambda b, i: (b * nq + i, 0))],
        out_specs=pl.BlockSpec((tq, bw), lambda b, i: (b * nq + i, 0)),
        scratch_shapes=[pltpu.VMEM((seq, tq), I32), pltpu.VMEM((heads, HEAD_DIM, tq), F32)],
        compiler_params=_cparams("arbitrary", "arbitrary"), name="dsa_attention",
    )(qcr, kcr, vct, qiz, kwb, kw)


def _merge_kernel(a_ref, b_ref, c_ref, w_ref, ga_ref, gb_ref, gc_ref, o_ref, *, chunks):
    rc = o_ref.shape[0] // chunks
    w = [w_ref[g].astype(BF16) for g in range(3)]
    for r in range(chunks):
        rs = slice(r * rc, (r + 1) * rc)
        acc = jax.nn.sigmoid(ga_ref[rs, :].astype(F32)) * _dot(a_ref[rs, :], w[0])
        acc = acc + jax.nn.sigmoid(gb_ref[rs, :].astype(F32)) * _dot(b_ref[rs, :], w[1])
        acc = acc + jax.nn.sigmoid(gc_ref[rs, :].astype(F32)) * _dot(c_ref[rs, :], w[2])
        o_ref[rs, :] = acc.astype(o_ref.dtype)


def _merge(out_a, out_b, out_c, w_branch, layer, gates):
    n, bw = out_a.shape
    d = w_branch.shape[3]
    tm, tn = _tile(n, 1024), _tile(d, 512)
    gstep = d // tn
    br = _resident_rows(tm, bw)

    def gate(g):
        return pl.BlockSpec((tm, tn), lambda i, j: (i, g * gstep + j))

    return pl.pallas_call(
        functools.partial(_merge_kernel, chunks=2),
        out_shape=jax.ShapeDtypeStruct((n, d), BF16),
        grid=(n // tm, d // tn),
        in_specs=[br, br, br, pl.BlockSpec((None, 3, bw, tn), lambda i, j: (layer, 0, 0, j)),
                  gate(0), gate(1), gate(2)],
        out_specs=pl.BlockSpec((tm, tn), lambda i, j: (i, j)),
        compiler_params=_cparams("arbitrary", "arbitrary"), name="branch_merge",
    )(out_a, out_b, out_c, w_branch, gates, gates, gates)


def _ffn_up_kernel(a_ref, wg_ref, wv_ref, cg_ref, cv_ref, o_ref, carry_ref, *, tpb, chunks):
    i, j = pl.program_id(0), pl.program_id(1)
    tm, tn = o_ref.shape
    rc = tm // chunks
    width = cg_ref.shape[0]
    w = jnp.concatenate([wg_ref[...].astype(BF16), wv_ref[...].astype(BF16)], axis=1)
    cw = jnp.concatenate([cg_ref[...], cv_ref[...]], axis=1)

    @pl.when(i % tpb == 0)
    def _():
        carry_ref[j] = jnp.zeros(carry_ref.shape[1:], F32)

    tail = carry_ref[j]
    for r in range(chunks):
        p = _dot(a_ref[r * rc:(r + 1) * rc, :], w)
        ext = jnp.concatenate([tail, p], axis=0)
        out = cw[width - 1:width, :] * p
        for k in range(width - 1):
            sh = width - 1 - k
            out = out + cw[k:k + 1, :] * ext[8 - sh:8 - sh + rc, :]
        tail = p[rc - 8:, :]
        g, v = out[:, :tn], out[:, tn:]
        o_ref[r * rc:(r + 1) * rc, :] = (g * jax.nn.sigmoid(g) * v).astype(o_ref.dtype)
    carry_ref[j] = tail


def _ffn_up(h, w_up, layer, conv_w, seq):
    n, d = h.shape
    width, f2 = conv_w.shape
    f = f2 // 2
    assert width - 1 <= 8
    tm, tn = _tile(seq, 2048), _tile(f, 256)
    tpb, nj = seq // tm, f // tn
    return pl.pallas_call(
        functools.partial(_ffn_up_kernel, tpb=tpb, chunks=8),
        out_shape=jax.ShapeDtypeStruct((n, f), BF16),
        grid=(n // tm, nj),
        in_specs=[_resident_rows(tm, d),
                  pl.BlockSpec((None, d, tn), lambda i, j: (layer, 0, j)),
                  pl.BlockSpec((None, d, tn), lambda i, j: (layer, 0, nj + j)),
                  pl.BlockSpec((width, tn), lambda i, j: (0, j)),
                  pl.BlockSpec((width, tn), lambda i, j: (0, nj + j))],
        out_specs=pl.BlockSpec((tm, tn), lambda i, j: (i, j)),
        scratch_shapes=[pltpu.VMEM((nj, 8, 2 * tn), F32)],
        compiler_params=_cparams("arbitrary", "arbitrary"), name="ffn_up_conv_gate",
    )(h, w_up, w_up, conv_w, conv_w)


def _layout(bw, ch):
    lay = {"bw": bw, "qa": 0, "ka": bw, "va": 2 * bw, "glu": 3 * bw}
    lay["qc"] = 3 * bw + 2 * ch
    lay["kc"] = lay["qc"] + bw
    lay["vc"] = lay["kc"] + bw
    lay["qi"] = lay["vc"] + bw
    lay["gl"] = lay["qi"] + IDX_HEADS * IDX_DIM
    return lay


def kernel(x, c, positions, ada_w, ada_b, norm_pre_mix, norm_post_mix, norm_pre_ffn, norm_post_ffn,
           w_in, conv_w, conv_b, conv_ln_g, conv_ln_b, w_branch, w_out, w_up, ffn_conv_w, w_down):
    batch, seq, d = x.shape
    depth = ada_w.shape[0]
    bw, ch = w_branch.shape[2], conv_w.shape[2]
    assert bw == ch and bw % HEAD_DIM == 0 and batch <= 8
    n = batch * seq
    k_top = min(TOPK_MAX, seq // 4)
    lay = _layout(bw, ch)
    small0 = lay["gl"]
    small_w = IDX_DIM + IDX_HEADS

    tabs = _rope_tables(positions)
    c8 = jnp.zeros((8, d), F32).at[:batch].set(c)
    mod = _modulation(c8, ada_w, ada_b)
    mod = mod[:, :batch].reshape(depth, batch, N_MOD, d).transpose(0, 2, 1, 3)
    mod = mod.reshape(depth, N_MOD, batch, 1, d)
    SH_M, SC_M, G_M, SH_F, SC_F, G_F = range(N_MOD)

    xf = x.reshape(n, d)
    h = _prenorm(xf, norm_pre_mix[0], mod[0], SC_M, SH_M, seq)
    for l in range(depth):
        w_gate = w_in[l, :, small0 + small_w:].astype(BF16)
        w_small = jnp.pad(w_in[l, :, small0:small0 + small_w], ((0, 0), (0, V7X_LANES - small_w))).astype(BF16)

        proj = _matmul(h, w_in, l, small0, BF16, "inproj_main")
        gates = _matmul(h, w_gate, None, w_gate.shape[1], BF16, "inproj_gates")
        kw, kwb = _kiwi(h, w_small, tabs[2], tabs[3])
        vat, vct, qcr, kcr, qiz = _prep(proj, tabs, lay, batch, seq)
        out_a = _sb_attention(proj, vat, lay, batch, seq)
        out_b = _conformer(proj, conv_w[l], conv_b[l], conv_ln_g[l], conv_ln_b[l], lay, seq)
        out_c = _dsa_attention(qcr, kcr, vct, qiz, kwb, kw, batch, seq, k_top)
        merged = _merge(out_a, out_b, out_c, w_branch, l, gates)
        y = _matmul(merged, w_out, l, d, F32, "outproj")
        xf, h = _post(y, xf, norm_post_mix[l], mod[l], G_M, seq,
                      nxt=(norm_pre_ffn[l], mod[l], SC_F, SH_F))

        act = _ffn_up(h, w_up, l, ffn_conv_w[l], seq)
        y = _matmul_kacc(act, w_down[l].astype(BF16), "ffn_down")
        nxt = (norm_pre_mix[l + 1], mod[l + 1], SC_M, SH_M) if l + 1 < depth else None
        xf, h = _post(y, xf, norm_post_ffn[l], mod[l], G_F, seq, nxt=nxt)
    return xf.reshape(batch, seq, d)
```

```python
import functools

import jax
import jax.numpy as jnp
from jax import lax
from jax.experimental import pallas as pl
from jax.experimental.pallas import tpu as pltpu

HEAD_DIM = 128
IDX_HEADS = 16
IDX_DIM = 64
TOPK_MAX = 256
ROPE_THETA = 500000.0
ROT_DIM = HEAD_DIM // 4
IDX_ROT_DIM = IDX_DIM // 4
N_MOD = 6
EPS = 1e-6

V7X_LANES = 128
V7X_VMEM_BYTES = 64 * 1024 * 1024
V7X_VMEM_LIMIT = V7X_VMEM_BYTES - 6 * 1024 * 1024

CONV_HALO = 32
MASKED = -1e30
LOG2E = 1.4426950408889634
HEAD_GROUP = 8
SB_DEAD = -120.0

F32 = jnp.float32
BF16 = jnp.bfloat16
I32 = jnp.int32
INT_MIN = -2 ** 31


def _cparams(*sem):
    return pltpu.CompilerParams(dimension_semantics=sem, vmem_limit_bytes=V7X_VMEM_LIMIT)


def _tile(dim, pref):
    t = min(dim, pref)
    while dim % t:
        t //= 2
    return t


def _iota(shape, axis):
    return lax.broadcasted_iota(I32, shape, axis)


def _rms(x):
    return x * lax.rsqrt(jnp.mean(x * x, axis=-1, keepdims=True) + EPS)


def _dot(a, b):
    return jnp.dot(a, b, preferred_element_type=F32)


def _dot_nt(a, b):
    return lax.dot_general(a, b, (((1,), (1,)), ((), ())), preferred_element_type=F32)


def _mod_kernel(c_ref, w_ref, b_ref, o_ref):
    c = c_ref[...]
    cond = c * jax.nn.sigmoid(c)
    o_ref[...] = _dot(cond.astype(BF16), w_ref[...].astype(BF16)) + b_ref[...]


def _modulation(c8, ada_w, ada_b):
    L, D, M = ada_w.shape
    tn = _tile(M, 1024)
    return pl.pallas_call(
        _mod_kernel,
        out_shape=jax.ShapeDtypeStruct((L, 8, M), F32),
        grid=(L, M // tn),
        in_specs=[pl.BlockSpec((8, D), lambda l, j: (0, 0)),
                  pl.BlockSpec((None, D, tn), lambda l, j: (l, 0, j)),
                  pl.BlockSpec((None, 1, tn), lambda l, j: (l, 0, j))],
        out_specs=pl.BlockSpec((None, 8, tn), lambda l, j: (l, 0, j)),
        compiler_params=_cparams("arbitrary", "arbitrary"),
        name="adaln_mod",
    )(c8, ada_w, ada_b.reshape(L, 1, M))


def _rope_tab_kernel(pos_ref, f_ref, s_ref, ca_ref, sa_ref, ci_ref, si_ref):
    p = pos_ref[...].astype(F32)
    ang = p * f_ref[0:1, :]
    ca_ref[...] = jnp.cos(ang)
    sa_ref[...] = jnp.sin(ang) * s_ref[0:1, :]
    ang = p * f_ref[1:2, :]
    ci_ref[...] = jnp.cos(ang)
    si_ref[...] = jnp.sin(ang) * s_ref[1:2, :]


def _rope_tables(positions):
    n = positions.size
    lane = jnp.arange(V7X_LANES)

    def rows(rot, period):
        inv = 1.0 / jnp.power(ROPE_THETA, jnp.arange(0, rot, 2, dtype=F32) / rot)
        lp = lane % period
        half = rot // 2
        f = jnp.where(lp < rot, inv[lp % half], 0.0)
        s = jnp.where(lp < half, -1.0, jnp.where(lp < rot, 1.0, 0.0))
        return f, s

    fa, sa = rows(ROT_DIM, HEAD_DIM)
    fi, si = rows(IDX_ROT_DIM, IDX_DIM)
    f = jnp.zeros((8, V7X_LANES), F32).at[0].set(fa).at[1].set(fi)
    s = jnp.zeros((8, V7X_LANES), F32).at[0].set(sa).at[1].set(si)
    posb = jnp.broadcast_to(positions.reshape(n, 1), (n, V7X_LANES))
    tm = _tile(n, 1024)
    spec = pl.BlockSpec((tm, V7X_LANES), lambda i: (i, 0))
    cst = pl.BlockSpec((8, V7X_LANES), lambda i: (0, 0))
    out = jax.ShapeDtypeStruct((n, V7X_LANES), F32)
    return pl.pallas_call(
        _rope_tab_kernel, out_shape=(out,) * 4, grid=(n // tm,),
        in_specs=[spec, cst, cst], out_specs=(spec,) * 4,
        compiler_params=_cparams("arbitrary"), name="rope_tables",
    )(posb, f, s)


def _rope(x, cos, sin, half, period):
    lane = _iota(x.shape, 1) % period
    partner = jnp.where(lane < half,
                        pltpu.roll(x, V7X_LANES - half, 1),
                        pltpu.roll(x, half, 1))
    return x * cos + partner * sin


def _prenorm_kernel(x_ref, g_ref, sc_ref, sh_ref, h_ref):
    y = _rms(x_ref[...]) * g_ref[...]
    h_ref[...] = (y * (1.0 + sc_ref[...]) + sh_ref[...]).astype(h_ref.dtype)


def _mod_spec(d, which, tpb):
    return pl.BlockSpec((None, None, 1, d), lambda i: (which, i // tpb, 0, 0))


def _prenorm(x, g, mod_l, which_sc, which_sh, seq):
    n, d = x.shape
    tm = _tile(seq, 256)
    tpb = seq // tm
    return pl.pallas_call(
        _prenorm_kernel,
        out_shape=jax.ShapeDtypeStruct((n, d), BF16),
        grid=(n // tm,),
        in_specs=[pl.BlockSpec((tm, d), lambda i: (i, 0)),
                  pl.BlockSpec((1, d), lambda i: (0, 0)),
                  _mod_spec(d, which_sc, tpb), _mod_spec(d, which_sh, tpb)],
        out_specs=pl.BlockSpec((tm, d), lambda i: (i, 0)),
        compiler_params=_cparams("arbitrary"), name="prenorm",
    )(x, g.reshape(1, d), mod_l, mod_l)


def _post_kernel(y_ref, x_ref, gp_ref, gate_ref, *rest, with_next):
    xn = x_ref[...] + gate_ref[...] * (_rms(y_ref[...]) * gp_ref[...])
    if with_next:
        gn_ref, sc_ref, sh_ref, xo_ref, h_ref = rest
        xo_ref[...] = xn
        h_ref[...] = (_rms(xn) * gn_ref[...] * (1.0 + sc_ref[...]) + sh_ref[...]).astype(h_ref.dtype)
    else:
        (xo_ref,) = rest
        xo_ref[...] = xn


def _post(y, x, g_post, mod_l, which_gate, seq, nxt=None):
    n, d = x.shape
    tm = _tile(seq, 256)
    tpb = seq // tm
    row = pl.BlockSpec((tm, d), lambda i: (i, 0))
    vec = pl.BlockSpec((1, d), lambda i: (0, 0))
    ins = [y, x, g_post.reshape(1, d), mod_l]
    specs = [row, row, vec, _mod_spec(d, which_gate, tpb)]
    outs = [jax.ShapeDtypeStruct((n, d), F32)]
    ospecs = [row]
    if nxt is not None:
        g_next, mod_n, which_sc, which_sh = nxt
        ins += [g_next.reshape(1, d), mod_n, mod_n]
        specs += [vec, _mod_spec(d, which_sc, tpb), _mod_spec(d, which_sh, tpb)]
        outs.append(jax.ShapeDtypeStruct((n, d), BF16))
        ospecs.append(row)
    res = pl.pallas_call(
        functools.partial(_post_kernel, with_next=nxt is not None),
        out_shape=tuple(outs), grid=(n // tm,), in_specs=specs, out_specs=tuple(ospecs),
        compiler_params=_cparams("arbitrary"), name="post_norm_residual",
    )(*ins)
    return res if nxt is not None else (res[0], None)


LHS_SPLIT = 1


def _resident_rows(tm, k):
    return pl.BlockSpec((tm, k), lambda i, j: (i, 0), pipeline_mode=pl.Buffered(1))


def _lhs_slabs(tm, k):
    assert k % (LHS_SPLIT * V7X_LANES) == 0
    return [pl.BlockSpec((tm, k // LHS_SPLIT), lambda i, j, c=c: (i, c), pipeline_mode=pl.Buffered(1))
            for c in range(LHS_SPLIT)]


def _mm_kernel(*refs):
    a_refs, w_ref, o_ref = refs[:LHS_SPLIT], refs[LHS_SPLIT], refs[LHS_SPLIT + 1]
    kc = a_refs[0].shape[1]
    acc = None
    for c, a_ref in enumerate(a_refs):
        part = _dot(a_ref[...], w_ref[c * kc:(c + 1) * kc, :].astype(BF16))
        acc = part if acc is None else acc + part
    o_ref[...] = acc.astype(o_ref.dtype)


def _matmul(a, w, layer, n, out_dtype, name):
    m, k = a.shape
    tm, tn = _tile(m, 2048), _tile(n, 512)
    return pl.pallas_call(
        _mm_kernel,
        out_shape=jax.ShapeDtypeStruct((m, n), out_dtype),
        grid=(m // tm, n // tn),
        in_specs=_lhs_slabs(tm, k) + [pl.BlockSpec((None, k, tn), lambda i, j: (layer, 0, j))],
        out_specs=pl.BlockSpec((tm, tn), lambda i, j: (i, j)),
        compiler_params=_cparams("arbitrary", "arbitrary"), name=name,
    )(*([a] * LHS_SPLIT), w)


def _mm_nt_kernel(*refs):
    a_refs, wt_ref, o_ref = refs[:LHS_SPLIT], refs[LHS_SPLIT], refs[LHS_SPLIT + 1]
    kc = a_refs[0].shape[1]
    acc = None
    for c, a_ref in enumerate(a_refs):
        part = _dot_nt(a_ref[...], wt_ref[0, :, c * kc:(c + 1) * kc].astype(BF16))
        acc = part if acc is None else acc + part
    o_ref[...] = acc.astype(o_ref.dtype)


def _matmul_nt(a, wt, layer, row0, n, out_dtype, name):
    m, k = a.shape
    tm, tn = _tile(m, 2048), _tile(n, 512)
    assert row0 % 8 == 0
    return pl.pallas_call(
        _mm_nt_kernel,
        out_shape=jax.ShapeDtypeStruct((m, n), out_dtype),
        grid=(m // tm, n // tn),
        in_specs=_lhs_slabs(tm, k) + [pl.BlockSpec((pl.Element(1), pl.Element(tn), pl.Element(k)),
                                                   lambda i, j: (layer, pl.multiple_of(row0 + j * tn, 8), 0))],
        out_specs=pl.BlockSpec((tm, tn), lambda i, j: (i, j)),
        compiler_params=_cparams("arbitrary", "arbitrary"), name=name,
    )(*([a] * LHS_SPLIT), wt)


def _mm_acc_kernel(a_ref, w_ref, o_ref):
    @pl.when(pl.program_id(2) == 0)
    def _():
        o_ref[...] = jnp.zeros(o_ref.shape, F32)

    o_ref[...] += _dot(a_ref[...], w_ref[...])


def _matmul_kacc(a, w, layer, name):
    m, k = a.shape
    n = w.shape[2]
    tm, tn, tk = _tile(m, 1024), _tile(n, 2048), _tile(k, 2048)
    return pl.pallas_call(
        _mm_acc_kernel,
        out_shape=jax.ShapeDtypeStruct((m, n), F32),
        grid=(m // tm, n // tn, k // tk),
        in_specs=[pl.BlockSpec((tm, tk), lambda i, j, kk: (i, kk)),
                  pl.BlockSpec((None, tk, tn), lambda i, j, kk: (layer, kk, j))],
        out_specs=pl.BlockSpec((tm, tn), lambda i, j, kk: (i, j)),
        compiler_params=_cparams("arbitrary", "arbitrary", "arbitrary"), name=name,
    )(a, w)


def _kiwi_kernel(a_ref, wt_ref, ci_ref, si_ref, o_ref, ob_ref):
    x = _dot_nt(a_ref[...], wt_ref[0].astype(BF16))
    lane = _iota(x.shape, 1)
    x = jnp.where(lane < IDX_DIM + IDX_HEADS, x, 0.0)
    is_k = lane < IDX_DIM
    c = jnp.where(is_k, ci_ref[...], 1.0)
    s = jnp.where(is_k, si_ref[...], 0.0)
    y = _rope(x, c, s, IDX_ROT_DIM // 2, IDX_DIM)
    wscale = (IDX_HEADS ** -0.5) * (IDX_DIM ** -0.5)
    y = y * jnp.where(is_k, 1.0, wscale)
    o_ref[...] = y
    ob_ref[...] = y.astype(BF16)


def _kiwi(h, wt, layer, row0, ci, si):
    n, k = h.shape
    tm = _tile(n, 1024)
    assert row0 % 8 == 0 and row0 + V7X_LANES <= wt.shape[1]
    row = pl.BlockSpec((tm, V7X_LANES), lambda i: (i, 0))
    return pl.pallas_call(
        _kiwi_kernel,
        out_shape=(jax.ShapeDtypeStruct((n, V7X_LANES), F32), jax.ShapeDtypeStruct((n, V7X_LANES), BF16)),
        grid=(n // tm,),
        in_specs=[pl.BlockSpec((tm, k), lambda i: (i, 0)),
                  pl.BlockSpec((pl.Element(1), pl.Element(V7X_LANES), pl.Element(k)),
                               lambda i: (layer, row0, 0)), row, row],
        out_specs=(row, row),
        compiler_params=_cparams("arbitrary"), name="inproj_indexer_kw",
    )(h, wt, ci, si)


def _prep_kernel(va_ref, qc_ref, kc_ref, vc_ref, qi_ref, ca_ref, sa_ref, ci_ref, si_ref,
                 vat_ref, vct_ref, qcr_ref, kcr_ref, qiz_ref):
    heads = va_ref.shape[1] // HEAD_DIM
    ca, sa, ci, si = ca_ref[...], sa_ref[...], ci_ref[...], si_ref[...]
    for h in range(heads):
        sl = slice(h * HEAD_DIM, (h + 1) * HEAD_DIM)
        vat_ref[h] = va_ref[:, sl].astype(F32).T.astype(BF16)
        vct_ref[h] = vc_ref[:, sl].astype(F32).T.astype(BF16)
        qcr_ref[:, sl] = _rope(qc_ref[:, sl].astype(F32), ca, sa, ROT_DIM // 2, HEAD_DIM).astype(BF16)
        kcr_ref[:, sl] = _rope(kc_ref[:, sl].astype(F32), ca, sa, ROT_DIM // 2, HEAD_DIM).astype(BF16)
    lane = _iota(ci.shape, 1)
    for t in range(IDX_HEADS * IDX_DIM // V7X_LANES):
        sl = slice(t * V7X_LANES, (t + 1) * V7X_LANES)
        y = _rope(qi_ref[:, sl].astype(F32), ci, si, IDX_ROT_DIM // 2, IDX_DIM)
        even = jnp.where(lane < IDX_DIM, y, 0.0)
        odd = jnp.where(lane < IDX_DIM, pltpu.roll(y, IDX_DIM, 1), 0.0)
        qiz_ref[:, 2 * t * V7X_LANES:(2 * t + 1) * V7X_LANES] = even.astype(BF16)
        qiz_ref[:, (2 * t + 1) * V7X_LANES:(2 * t + 2) * V7X_LANES] = odd.astype(BF16)


def _prep(proj, tabs, lay, batch, seq):
    n = proj.shape[0]
    bw, heads = lay["bw"], lay["bw"] // HEAD_DIM
    qw = IDX_HEADS * IDX_DIM
    tm = _tile(seq, 512)
    tpb = seq // tm

    def cols(off, width):
        return pl.BlockSpec((tm, width), lambda i: (i, off // width))

    tab = pl.BlockSpec((tm, V7X_LANES), lambda i: (i, 0))
    vt = pl.BlockSpec((None, heads, HEAD_DIM, tm), lambda i: (i // tpb, 0, 0, i % tpb))
    rowb = pl.BlockSpec((tm, bw), lambda i: (i, 0))
    ca, sa, ci, si = tabs
    return pl.pallas_call(
        _prep_kernel,
        out_shape=(jax.ShapeDtypeStruct((batch, heads, HEAD_DIM, seq), BF16),
                   jax.ShapeDtypeStruct((batch, heads, HEAD_DIM, seq), BF16),
                   jax.ShapeDtypeStruct((n, bw), BF16),
                   jax.ShapeDtypeStruct((n, bw), BF16),
                   jax.ShapeDtypeStruct((n, 2 * qw), BF16)),
        grid=(n // tm,),
        in_specs=[cols(lay["va"], bw), cols(lay["qc"], bw), cols(lay["kc"], bw), cols(lay["vc"], bw),
                  cols(lay["qi"], qw), tab, tab, tab, tab],
        out_specs=(vt, vt, rowb, rowb, pl.BlockSpec((tm, 2 * qw), lambda i: (i, 0))),
        compiler_params=_cparams("arbitrary"), name="attn_prep",
    )(proj, proj, proj, proj, proj, ca, sa, ci, si)


def _sb_kernel(q_ref, k_ref, vt_ref, o_ref, acc_ref, *, tq, scale):
    i = pl.program_id(1)
    heads = q_ref.shape[1] // HEAD_DIM
    row, col = _iota((tq, tq), 0), _iota((tq, tq), 1)
    after = jnp.where(col > row, 1.0, 0.0).astype(BF16)
    causal = row < col

    def all_heads(s0, carry, diag):
        hr = range(heads)
        hs = [slice(h * HEAD_DIM, (h + 1) * HEAD_DIM) for h in hr]
        zs = [_dot_nt(k_ref[pl.ds(s0, tq), hs[h]], q_ref[:, hs[h]]) for h in hr]
        lss, l1ms, his, los = [], [], [], []
        for h in hr:
            z = zs[h] * scale
            ls = jnp.minimum(z, 0.0) - jnp.log(1.0 + jnp.exp(-jnp.abs(z)))
            l1m = ls - z
            if diag:
                l1m = jnp.where(causal, l1m, 0.0)
            hi = l1m.astype(BF16)
            lss.append(ls)
            l1ms.append(l1m)
            his.append(hi)
            los.append((l1m - hi.astype(F32)).astype(BF16))
        excls = [_dot(after, his[h]) + _dot(after, los[h]) for h in hr]
        ps = []
        for h in hr:
            a = jnp.exp(lss[h] + excls[h] + carry[h:h + 1])
            if diag:
                a = jnp.where(causal, a, 0.0)
            ps.append(a.astype(BF16))
        pvs = [_dot(vt_ref[h, :, pl.ds(s0, tq)], ps[h]) for h in hr]
        for h in hr:
            acc_ref[h] = pvs[h] if diag else acc_ref[h] + pvs[h]
        return carry + jnp.concatenate([jnp.sum(l1ms[h], axis=0, keepdims=True) for h in hr], axis=0)

    def alive(carry):
        return (jnp.max(carry) > SB_DEAD).astype(I32)

    carry = all_heads(pl.multiple_of(i * tq, tq), jnp.zeros((heads, tq), F32), True)

    def body(st):
        j, _, carry = st
        carry = all_heads(pl.multiple_of((i - 1 - j) * tq, tq), carry, False)
        return j + 1, alive(carry), carry

    lax.while_loop(lambda st: (st[0] < i) & (st[1] > 0), body, (jnp.int32(0), alive(carry), carry))
    for h in range(heads):
        o_ref[:, h * HEAD_DIM:(h + 1) * HEAD_DIM] = acc_ref[h].T.astype(o_ref.dtype)


def _sb_attention(proj, vat, lay, batch, seq):
    n = proj.shape[0]
    bw, heads = lay["bw"], lay["bw"] // HEAD_DIM
    tq = _tile(seq, 256)
    nq = seq // tq
    qb, kb = lay["qa"] // bw, lay["ka"] // bw
    return pl.pallas_call(
        functools.partial(_sb_kernel, tq=tq, scale=HEAD_DIM ** -0.5),
        out_shape=jax.ShapeDtypeStruct((n, bw), BF16),
        grid=(batch, nq),
        in_specs=[pl.BlockSpec((tq, bw), lambda b, i: (b * nq + i, qb)),
                  pl.BlockSpec((seq, bw), lambda b, i: (b, kb)),
                  pl.BlockSpec((None, heads, HEAD_DIM, seq), lambda b, i: (b, 0, 0, 0))],
        out_specs=pl.BlockSpec((tq, bw), lambda b, i: (b * nq + i, 0)),
        scratch_shapes=[pltpu.VMEM((heads, HEAD_DIM, tq), F32)],
        compiler_params=_cparams("arbitrary", "arbitrary"), name="sb_attention",
    )(proj, proj, vat)


def _conformer_kernel(a_ref, g_ref, ha_ref, hg_ref, cw_ref, cb_ref, lg_ref, lb_ref, o_ref,
                      u_ref, y_ref, *, tm, tpb, rc, cc):
    width, ch = cw_ref.shape
    first = (pl.program_id(0) % tpb) == 0
    hu = ha_ref[...].astype(F32) * jax.nn.sigmoid(hg_ref[...].astype(F32))
    u_ref[0:CONV_HALO, :] = jnp.where(first, 0.0, hu)
    u_ref[CONV_HALO:, :] = a_ref[...].astype(F32) * jax.nn.sigmoid(g_ref[...].astype(F32))
    lead = CONV_HALO - (width - 1)

    def rows(r, _):
        r0 = pl.multiple_of(r * rc, rc)
        for c in range(ch // cc):
            cs = slice(c * cc, (c + 1) * cc)
            win = u_ref[pl.ds(r0, rc + CONV_HALO), cs]
            acc = jnp.zeros((rc, cc), F32)
            for k in range(width):
                acc = acc + cw_ref[k:k + 1, cs] * win[lead + k:lead + k + rc, :]
            y_ref[:, cs] = acc + cb_ref[:, cs]
        y = y_ref[...]
        mu = jnp.mean(y, axis=-1, keepdims=True)
        yc = y - mu
        var = jnp.mean(yc * yc, axis=-1, keepdims=True)
        yn = yc * lax.rsqrt(var + EPS) * lg_ref[...] + lb_ref[...]
        o_ref[pl.ds(r0, rc), :] = (yn * jax.nn.sigmoid(yn)).astype(o_ref.dtype)
        return 0

    lax.fori_loop(0, tm // rc, rows, 0)


def _conformer(proj, conv_w, conv_b, ln_g, ln_b, lay, seq):
    n = proj.shape[0]
    width, ch = conv_w.shape
    assert width - 1 <= CONV_HALO
    tm = _tile(seq, 512)
    tpb = seq // tm
    rc, cc = _tile(tm, 64), _tile(ch, 256)
    a_blk, hb = lay["glu"] // ch, tm // CONV_HALO
    vec = pl.BlockSpec((1, ch), lambda i: (0, 0))
    return pl.pallas_call(
        functools.partial(_conformer_kernel, tm=tm, tpb=tpb, rc=rc, cc=cc),
        out_shape=jax.ShapeDtypeStruct((n, ch), BF16),
        grid=(n // tm,),
        in_specs=[pl.BlockSpec((tm, ch), lambda i: (i, a_blk)),
                  pl.BlockSpec((tm, ch), lambda i: (i, a_blk + 1)),
                  pl.BlockSpec((CONV_HALO, ch), lambda i: (jnp.maximum(i * hb - 1, 0), a_blk)),
                  pl.BlockSpec((CONV_HALO, ch), lambda i: (jnp.maximum(i * hb - 1, 0), a_blk + 1)),
                  pl.BlockSpec((width, ch), lambda i: (0, 0)), vec, vec, vec],
        out_specs=pl.BlockSpec((tm, ch), lambda i: (i, 0)),
        scratch_shapes=[pltpu.VMEM((tm + CONV_HALO, ch), F32), pltpu.VMEM((rc, ch), F32)],
        compiler_params=_cparams("arbitrary"), name="conformer_conv",
    )(proj, proj, proj, proj, conv_w, conv_b.reshape(1, ch), ln_g.reshape(1, ch), ln_b.reshape(1, ch))


def _dsa_kernel(qc_ref, kc_ref, vt_ref, qiz_ref, kb_ref, kwq_ref, o_ref, key_ref, acc_ref, *, tq, k_top, scale):
    i = pl.program_id(1)
    heads = qc_ref.shape[1] // HEAD_DIM
    seq_bits = max(1, (kc_ref.shape[0] - 1).bit_length())
    row, col = _iota((tq, tq), 0), _iota((tq, tq), 1)
    admissible = row <= col
    wi_t = kwq_ref[...].T

    def start(c):
        return pl.multiple_of(c * tq, tq)

    def score_tile(c, diag):
        kb = kb_ref[pl.ds(start(c), tq), :]
        acc = jnp.zeros((tq, tq), F32)
        for h in range(IDX_HEADS):
            d = _dot_nt(kb, qiz_ref[:, h * V7X_LANES:(h + 1) * V7X_LANES])
            acc = acc + jnp.maximum(d, 0.0) * wi_t[IDX_DIM + h:IDX_DIM + h + 1, :]
        if diag:
            acc = jnp.where(admissible, acc, -jnp.inf)
        bits = lax.bitcast_convert_type(acc, I32)
        key_ref[pl.ds(start(c), tq), :] = bits ^ (lax.shift_right_arithmetic(bits, 31) & 0x7FFFFFFF)

    def score_body(c, _):
        score_tile(c, False)
        return 0

    lax.fori_loop(0, i, score_body, 0)
    score_tile(i, True)

    def count(pred):
        def body(c, cnt):
            hit = pred(key_ref[pl.ds(start(c), tq), :], c)
            ones = jnp.where(hit, 1.0, 0.0).reshape(tq // 8, 8, tq)
            return cnt + jnp.sum(ones, axis=0)

        def pair(cc, cnt):
            return body(2 * cc + 1, body(2 * cc, cnt))

        pairs = (i + 1) // 2
        part = lax.fori_loop(0, pairs, pair, jnp.zeros((8, tq), F32))
        part = lax.fori_loop(2 * pairs, i + 1, body, part)
        return jnp.sum(part, axis=0, keepdims=True)

    def bit_body(b, ans):
        cand = ans | lax.shift_left(jnp.int32(1), 31 - b)
        cnt = count(lambda k, c: k >= (cand ^ INT_MIN))
        return jnp.where(cnt >= k_top, cand, ans)

    thr = lax.fori_loop(0, 32, bit_body, jnp.zeros((1, tq), I32)) ^ INT_MIN

    n_ge = count(lambda k, c: k >= thr)

    @pl.when(jnp.max(n_ge) > k_top)
    def _():
        need = k_top - count(lambda k, c: k > thr)

        def pos_body(b, x):
            cand = x | lax.shift_left(jnp.int32(1), seq_bits - 1 - b)
            below = count(lambda k, c: (k == thr) & (row + c * tq < cand))
            return jnp.where(below < need, cand, x)

        last = lax.fori_loop(0, seq_bits, pos_body, jnp.zeros((1, tq), I32))

        def demote(c, _):
            k = key_ref[pl.ds(start(c), tq), :]
            drop = (k == thr) & (row + c * tq > last) & (n_ge > k_top)
            key_ref[pl.ds(start(c), tq), :] = jnp.where(drop, thr - 1, k)
            return 0

        lax.fori_loop(0, i + 1, demote, 0)

    def attend(c, st, diag):
        m, l = st
        s0 = start(c)
        sel = key_ref[pl.ds(s0, tq), :] >= thr
        if diag:
            sel = sel & admissible
        ms, ls = [], []
        for g0 in range(0, heads, HEAD_GROUP):
            attend_group(range(g0, min(g0 + HEAD_GROUP, heads)), s0, sel, m, l, ms, ls)
        return jnp.concatenate(ms, axis=0), jnp.concatenate(ls, axis=0)

    def attend_group(hr, s0, sel, m, l, ms, ls):
        hs = {h: slice(h * HEAD_DIM, (h + 1) * HEAD_DIM) for h in hr}
        lgs = {h: _dot_nt(kc_ref[pl.ds(s0, tq), hs[h]], qc_ref[:, hs[h]]) for h in hr}
        alphas, ps = {}, {}
        for h in hr:
            lgm = jnp.where(sel, lgs[h], MASKED)
            m_new = jnp.maximum(m[h:h + 1], jnp.max(lgm, axis=0, keepdims=True))
            alpha = jnp.exp2((m[h:h + 1] - m_new) * (scale * LOG2E))
            p = jnp.where(sel, jnp.exp2((lgm - m_new) * (scale * LOG2E)), 0.0)
            ls.append(alpha * l[h:h + 1] + jnp.sum(p, axis=0, keepdims=True))
            ms.append(m_new)
            alphas[h] = alpha
            ps[h] = p.astype(BF16)
        pvs = {h: _dot(vt_ref[h, :, pl.ds(s0, tq)], ps[h]) for h in hr}
        for h in hr:
            acc_ref[h] = alphas[h] * acc_ref[h] + pvs[h]

    acc_ref[...] = jnp.zeros(acc_ref.shape, F32)
    st = (jnp.full((heads, tq), MASKED, F32), jnp.zeros((heads, tq), F32))
    st = lax.fori_loop(0, i, lambda c, s: attend(c, s, False), st)
    _, l = attend(i, st, True)
    for h in range(heads):
        o_ref[:, h * HEAD_DIM:(h + 1) * HEAD_DIM] = (acc_ref[h] / l[h:h + 1]).T.astype(o_ref.dtype)


def _dsa_attention(qcr, kcr, vct, qiz, kwb, kw, batch, seq, k_top):
    n, bw = qcr.shape
    heads = bw // HEAD_DIM
    tq = _tile(seq, 256)
    nq = seq // tq
    return pl.pallas_call(
        functools.partial(_dsa_kernel, tq=tq, k_top=k_top, scale=HEAD_DIM ** -0.5),
        out_shape=jax.ShapeDtypeStruct((n, bw), BF16),
        grid=(batch, nq),
        in_specs=[pl.BlockSpec((tq, bw), lambda b, i: (b * nq + i, 0)),
                  pl.BlockSpec((seq, bw), lambda b, i: (b, 0)),
                  pl.BlockSpec((None, heads, HEAD_DIM, seq), lambda b, i: (b, 0, 0, 0)),
                  pl.BlockSpec((tq, qiz.shape[1]), lambda b, i: (b * nq + i, 0)),
                  pl.BlockSpec((seq, V7X_LANES), lambda b, i: (b, 0)),
                  pl.BlockSpec((tq, V7X_LANES), lambda b, i: (b * nq + i, 0))],
        out_specs=pl.BlockSpec((tq, bw), lambda b, i: (b * nq + i, 0)),
        scratch_shapes=[pltpu.VMEM((seq, tq), I32), pltpu.VMEM((heads, HEAD_DIM, tq), F32)],
        compiler_params=_cparams("arbitrary", "arbitrary"), name="dsa_attention",
    )(qcr, kcr, vct, qiz, kwb, kw)


def _merge_kernel(a_ref, b_ref, c_ref, w_ref, ga_ref, gb_ref, gc_ref, o_ref, *, chunks):
    rc = o_ref.shape[0] // chunks
    w = [w_ref[g].astype(BF16) for g in range(3)]
    for r in range(chunks):
        rs = slice(r * rc, (r + 1) * rc)
        acc = jax.nn.sigmoid(ga_ref[rs, :].astype(F32)) * _dot(a_ref[rs, :], w[0])
        acc = acc + jax.nn.sigmoid(gb_ref[rs, :].astype(F32)) * _dot(b_ref[rs, :], w[1])
        acc = acc + jax.nn.sigmoid(gc_ref[rs, :].astype(F32)) * _dot(c_ref[rs, :], w[2])
        o_ref[rs, :] = acc.astype(o_ref.dtype)


def _merge(out_a, out_b, out_c, w_branch, layer, gates):
    n, bw = out_a.shape
    d = w_branch.shape[3]
    tm, tn = _tile(n, 2048), _tile(d, 512)
    gstep = d // tn
    br = _resident_rows(tm, bw)

    def gate(g):
        return pl.BlockSpec((tm, tn), lambda i, j: (i, g * gstep + j))

    return pl.pallas_call(
        functools.partial(_merge_kernel, chunks=4),
        out_shape=jax.ShapeDtypeStruct((n, d), BF16),
        grid=(n // tm, d // tn),
        in_specs=[br, br, br, pl.BlockSpec((None, 3, bw, tn), lambda i, j: (layer, 0, 0, j)),
                  gate(0), gate(1), gate(2)],
        out_specs=pl.BlockSpec((tm, tn), lambda i, j: (i, j)),
        compiler_params=_cparams("arbitrary", "arbitrary"), name="branch_merge",
    )(out_a, out_b, out_c, w_branch, gates, gates, gates)


def _ffn_up_kernel(*refs, tpb, chunks):
    a_refs = refs[:LHS_SPLIT]
    wg_ref, wv_ref, cg_ref, cv_ref, o_ref, carry_ref = refs[LHS_SPLIT:]
    i, j = pl.program_id(0), pl.program_id(1)
    tm, tn = o_ref.shape
    rc = tm // chunks
    kc = a_refs[0].shape[1]
    width = cg_ref.shape[0]
    w = jnp.concatenate([wg_ref[...].astype(BF16), wv_ref[...].astype(BF16)], axis=1)
    cw = jnp.concatenate([cg_ref[...], cv_ref[...]], axis=1)

    @pl.when(i % tpb == 0)
    def _():
        carry_ref[j] = jnp.zeros(carry_ref.shape[1:], F32)

    tail = carry_ref[j]
    for r in range(chunks):
        p = None
        for c, a_ref in enumerate(a_refs):
            part = _dot(a_ref[r * rc:(r + 1) * rc, :], w[c * kc:(c + 1) * kc, :])
            p = part if p is None else p + part
        ext = jnp.concatenate([tail, p], axis=0)
        out = cw[width - 1:width, :] * p
        for k in range(width - 1):
            sh = width - 1 - k
            out = out + cw[k:k + 1, :] * ext[8 - sh:8 - sh + rc, :]
        tail = p[rc - 8:, :]
        g, v = out[:, :tn], out[:, tn:]
        o_ref[r * rc:(r + 1) * rc, :] = (g * jax.nn.sigmoid(g) * v).astype(o_ref.dtype)
    carry_ref[j] = tail


def _ffn_up(h, w_up, layer, conv_w, seq):
    n, d = h.shape
    width, f2 = conv_w.shape
    f = f2 // 2
    assert width - 1 <= 8
    tm, tn = _tile(seq, 2048), _tile(f, 256)
    tpb, nj = seq // tm, f // tn
    return pl.pallas_call(
        functools.partial(_ffn_up_kernel, tpb=tpb, chunks=8),
        out_shape=jax.ShapeDtypeStruct((n, f), BF16),
        grid=(n // tm, nj),
        in_specs=_lhs_slabs(tm, d) + [
            pl.BlockSpec((None, d, tn), lambda i, j: (layer, 0, j)),
            pl.BlockSpec((None, d, tn), lambda i, j: (layer, 0, nj + j)),
            pl.BlockSpec((width, tn), lambda i, j: (0, j)),
            pl.BlockSpec((width, tn), lambda i, j: (0, nj + j))],
        out_specs=pl.BlockSpec((tm, tn), lambda i, j: (i, j)),
        scratch_shapes=[pltpu.VMEM((nj, 8, 2 * tn), F32)],
        compiler_params=_cparams("arbitrary", "arbitrary"), name="ffn_up_conv_gate",
    )(*([h] * LHS_SPLIT), w_up, w_up, conv_w, conv_w)


def _layout(bw, ch):
    lay = {"bw": bw, "qa": 0, "ka": bw, "va": 2 * bw, "glu": 3 * bw}
    lay["qc"] = 3 * bw + 2 * ch
    lay["kc"] = lay["qc"] + bw
    lay["vc"] = lay["kc"] + bw
    lay["qi"] = lay["vc"] + bw
    lay["gl"] = lay["qi"] + IDX_HEADS * IDX_DIM
    return lay


def kernel(x, c, positions, ada_w, ada_b, norm_pre_mix, norm_post_mix, norm_pre_ffn, norm_post_ffn,
           w_in, conv_w, conv_b, conv_ln_g, conv_ln_b, w_branch, w_out, w_up, ffn_conv_w, w_down):
    batch, seq, d = x.shape
    depth = ada_w.shape[0]
    bw, ch = w_branch.shape[2], conv_w.shape[2]
    assert bw == ch and bw % HEAD_DIM == 0 and batch <= 8
    n = batch * seq
    k_top = min(TOPK_MAX, seq // 4)
    lay = _layout(bw, ch)
    small0 = lay["gl"]
    small_w = IDX_DIM + IDX_HEADS

    tabs = _rope_tables(positions)
    c8 = jnp.zeros((8, d), F32).at[:batch].set(c)
    mod = _modulation(c8, ada_w, ada_b)
    mod = mod[:, :batch].reshape(depth, batch, N_MOD, d).transpose(0, 2, 1, 3)
    mod = mod.reshape(depth, N_MOD, batch, 1, d)
    SH_M, SC_M, G_M, SH_F, SC_F, G_F = range(N_MOD)

    w_in_t = jnp.swapaxes(w_in, 1, 2)
    w_down_b = w_down.astype(BF16)

    xf = x.reshape(n, d)
    h = _prenorm(xf, norm_pre_mix[0], mod[0], SC_M, SH_M, seq)
    for l in range(depth):
        proj = _matmul_nt(h, w_in_t, l, 0, small0, BF16, "inproj_main")
        gates = _matmul_nt(h, w_in_t, l, small0 + small_w, w_branch.shape[1] * d, BF16, "inproj_gates")
        kw, kwb = _kiwi(h, w_in_t, l, small0, tabs[2], tabs[3])
        vat, vct, qcr, kcr, qiz = _prep(proj, tabs, lay, batch, seq)
        out_a = _sb_attention(proj, vat, lay, batch, seq)
        out_b = _conformer(proj, conv_w[l], conv_b[l], conv_ln_g[l], conv_ln_b[l], lay, seq)
        out_c = _dsa_attention(qcr, kcr, vct, qiz, kwb, kw, batch, seq, k_top)
        merged = _merge(out_a, out_b, out_c, w_branch, l, gates)
        y = _matmul(merged, w_out, l, d, F32, "outproj")
        xf, h = _post(y, xf, norm_post_mix[l], mod[l], G_M, seq,
                      nxt=(norm_pre_ffn[l], mod[l], SC_F, SH_F))

        act = _ffn_up(h, w_up, l, ffn_conv_w[l], seq)
        y = _matmul_kacc(act, w_down_b, l, "ffn_down")
        nxt = (norm_pre_mix[l + 1], mod[l + 1], SC_M, SH_M) if l + 1 < depth else None
        xf, h = _post(y, xf, norm_post_ffn[l], mod[l], G_F, seq, nxt=nxt)
    return xf.reshape(batch, seq, d)
```

```python
import functools

import jax
import jax.numpy as jnp
from jax import lax
from jax.experimental import pallas as pl
from jax.experimental.pallas import tpu as pltpu

HEAD_DIM = 128
IDX_HEADS = 16
IDX_DIM = 64
TOPK_MAX = 256
ROPE_THETA = 500000.0
ROT_DIM = HEAD_DIM // 4
IDX_ROT_DIM = IDX_DIM // 4
N_MOD = 6
EPS = 1e-6

V7X_LANES = 128
V7X_VMEM_BYTES = 64 * 1024 * 1024
V7X_VMEM_LIMIT = V7X_VMEM_BYTES - 6 * 1024 * 1024

CONV_HALO = 32
MASKED = -1e30
LOG2E = 1.4426950408889634
HEAD_GROUP = 8
SB_DEAD = -120.0

F32 = jnp.float32
BF16 = jnp.bfloat16
I32 = jnp.int32
INT_MIN = -2 ** 31


def _cparams(*sem):
    return pltpu.CompilerParams(dimension_semantics=sem, vmem_limit_bytes=V7X_VMEM_LIMIT)


def _tile(dim, pref):
    t = min(dim, pref)
    while dim % t:
        t //= 2
    return t


def _iota(shape, axis):
    return lax.broadcasted_iota(I32, shape, axis)


def _rms(x):
    return x * lax.rsqrt(jnp.mean(x * x, axis=-1, keepdims=True) + EPS)


def _dot(a, b):
    return jnp.dot(a, b, preferred_element_type=F32)


def _dot_nt(a, b):
    return lax.dot_general(a, b, (((1,), (1,)), ((), ())), preferred_element_type=F32)


def _mod_kernel(c_ref, w_ref, b_ref, o_ref):
    c = c_ref[...]
    cond = c * jax.nn.sigmoid(c)
    o_ref[...] = _dot(cond.astype(BF16), w_ref[...].astype(BF16)) + b_ref[...]


def _modulation(c8, ada_w, ada_b):
    L, D, M = ada_w.shape
    tn = _tile(M, 1024)
    return pl.pallas_call(
        _mod_kernel,
        out_shape=jax.ShapeDtypeStruct((L, 8, M), F32),
        grid=(L, M // tn),
        in_specs=[pl.BlockSpec((8, D), lambda l, j: (0, 0)),
                  pl.BlockSpec((None, D, tn), lambda l, j: (l, 0, j)),
                  pl.BlockSpec((None, 1, tn), lambda l, j: (l, 0, j))],
        out_specs=pl.BlockSpec((None, 8, tn), lambda l, j: (l, 0, j)),
        compiler_params=_cparams("arbitrary", "arbitrary"),
        name="adaln_mod",
    )(c8, ada_w, ada_b.reshape(L, 1, M))


def _rope_tab_kernel(pos_ref, f_ref, s_ref, ca_ref, sa_ref, ci_ref, si_ref):
    p = pos_ref[...].astype(F32)
    ang = p * f_ref[0:1, :]
    ca_ref[...] = jnp.cos(ang)
    sa_ref[...] = jnp.sin(ang) * s_ref[0:1, :]
    ang = p * f_ref[1:2, :]
    ci_ref[...] = jnp.cos(ang)
    si_ref[...] = jnp.sin(ang) * s_ref[1:2, :]


def _rope_tables(positions):
    n = positions.size
    lane = jnp.arange(V7X_LANES)

    def rows(rot, period):
        inv = 1.0 / jnp.power(ROPE_THETA, jnp.arange(0, rot, 2, dtype=F32) / rot)
        lp = lane % period
        half = rot // 2
        f = jnp.where(lp < rot, inv[lp % half], 0.0)
        s = jnp.where(lp < half, -1.0, jnp.where(lp < rot, 1.0, 0.0))
        return f, s

    fa, sa = rows(ROT_DIM, HEAD_DIM)
    fi, si = rows(IDX_ROT_DIM, IDX_DIM)
    f = jnp.zeros((8, V7X_LANES), F32).at[0].set(fa).at[1].set(fi)
    s = jnp.zeros((8, V7X_LANES), F32).at[0].set(sa).at[1].set(si)
    posb = jnp.broadcast_to(positions.reshape(n, 1), (n, V7X_LANES))
    tm = _tile(n, 1024)
    spec = pl.BlockSpec((tm, V7X_LANES), lambda i: (i, 0))
    cst = pl.BlockSpec((8, V7X_LANES), lambda i: (0, 0))
    out = jax.ShapeDtypeStruct((n, V7X_LANES), F32)
    return pl.pallas_call(
        _rope_tab_kernel, out_shape=(out,) * 4, grid=(n // tm,),
        in_specs=[spec, cst, cst], out_specs=(spec,) * 4,
        compiler_params=_cparams("arbitrary"), name="rope_tables",
    )(posb, f, s)


def _rope(x, cos, sin, half, period):
    lane = _iota(x.shape, 1) % period
    partner = jnp.where(lane < half,
                        pltpu.roll(x, V7X_LANES - half, 1),
                        pltpu.roll(x, half, 1))
    return x * cos + partner * sin


def _prenorm_kernel(x_ref, g_ref, sc_ref, sh_ref, h_ref):
    y = _rms(x_ref[...]) * g_ref[...]
    h_ref[...] = (y * (1.0 + sc_ref[...]) + sh_ref[...]).astype(h_ref.dtype)


def _mod_spec(d, which, tpb):
    return pl.BlockSpec((None, None, 1, d), lambda i: (which, i // tpb, 0, 0))


def _prenorm(x, g, mod_l, which_sc, which_sh, seq):
    n, d = x.shape
    tm = _tile(seq, 256)
    tpb = seq // tm
    return pl.pallas_call(
        _prenorm_kernel,
        out_shape=jax.ShapeDtypeStruct((n, d), BF16),
        grid=(n // tm,),
        in_specs=[pl.BlockSpec((tm, d), lambda i: (i, 0)),
                  pl.BlockSpec((1, d), lambda i: (0, 0)),
                  _mod_spec(d, which_sc, tpb), _mod_spec(d, which_sh, tpb)],
        out_specs=pl.BlockSpec((tm, d), lambda i: (i, 0)),
        compiler_params=_cparams("arbitrary"), name="prenorm",
    )(x, g.reshape(1, d), mod_l, mod_l)


def _post_kernel(y_ref, x_ref, gp_ref, gate_ref, *rest, with_next):
    xn = x_ref[...] + gate_ref[...] * (_rms(y_ref[...]) * gp_ref[...])
    if with_next:
        gn_ref, sc_ref, sh_ref, xo_ref, h_ref = rest
        xo_ref[...] = xn
        h_ref[...] = (_rms(xn) * gn_ref[...] * (1.0 + sc_ref[...]) + sh_ref[...]).astype(h_ref.dtype)
    else:
        (xo_ref,) = rest
        xo_ref[...] = xn


def _post(y, x, g_post, mod_l, which_gate, seq, nxt=None):
    n, d = x.shape
    tm = _tile(seq, 256)
    tpb = seq // tm
    row = pl.BlockSpec((tm, d), lambda i: (i, 0))
    vec = pl.BlockSpec((1, d), lambda i: (0, 0))
    ins = [y, x, g_post.reshape(1, d), mod_l]
    specs = [row, row, vec, _mod_spec(d, which_gate, tpb)]
    outs = [jax.ShapeDtypeStruct((n, d), F32)]
    ospecs = [row]
    if nxt is not None:
        g_next, mod_n, which_sc, which_sh = nxt
        ins += [g_next.reshape(1, d), mod_n, mod_n]
        specs += [vec, _mod_spec(d, which_sc, tpb), _mod_spec(d, which_sh, tpb)]
        outs.append(jax.ShapeDtypeStruct((n, d), BF16))
        ospecs.append(row)
    res = pl.pallas_call(
        functools.partial(_post_kernel, with_next=nxt is not None),
        out_shape=tuple(outs), grid=(n // tm,), in_specs=specs, out_specs=tuple(ospecs),
        compiler_params=_cparams("arbitrary"), name="post_norm_residual",
    )(*ins)
    return res if nxt is not None else (res[0], None)


LHS_SPLIT = 1


def _resident_rows(tm, k):
    return pl.BlockSpec((tm, k), lambda i, j: (i, 0), pipeline_mode=pl.Buffered(1))


def _lhs_slabs(tm, k):
    assert k % (LHS_SPLIT * V7X_LANES) == 0
    return [pl.BlockSpec((tm, k // LHS_SPLIT), lambda i, j, c=c: (i, c), pipeline_mode=pl.Buffered(1))
            for c in range(LHS_SPLIT)]


def _mm_kernel(*refs):
    a_refs, w_ref, o_ref = refs[:LHS_SPLIT], refs[LHS_SPLIT], refs[LHS_SPLIT + 1]
    kc = a_refs[0].shape[1]
    acc = None
    for c, a_ref in enumerate(a_refs):
        part = _dot(a_ref[...], w_ref[c * kc:(c + 1) * kc, :].astype(BF16))
        acc = part if acc is None else acc + part
    o_ref[...] = acc.astype(o_ref.dtype)


def _matmul(a, w, layer, n, out_dtype, name):
    m, k = a.shape
    tm, tn = _tile(m, 2048), _tile(n, 512)
    return pl.pallas_call(
        _mm_kernel,
        out_shape=jax.ShapeDtypeStruct((m, n), out_dtype),
        grid=(m // tm, n // tn),
        in_specs=_lhs_slabs(tm, k) + [pl.BlockSpec((None, k, tn), lambda i, j: (layer, 0, j))],
        out_specs=pl.BlockSpec((tm, tn), lambda i, j: (i, j)),
        compiler_params=_cparams("arbitrary", "arbitrary"), name=name,
    )(*([a] * LHS_SPLIT), w)


def _mm_nt_kernel(*refs):
    a_refs, wt_ref, o_ref = refs[:LHS_SPLIT], refs[LHS_SPLIT], refs[LHS_SPLIT + 1]
    kc = a_refs[0].shape[1]
    acc = None
    for c, a_ref in enumerate(a_refs):
        part = _dot_nt(a_ref[...], wt_ref[0, :, c * kc:(c + 1) * kc].astype(BF16))
        acc = part if acc is None else acc + part
    o_ref[...] = acc.astype(o_ref.dtype)


def _matmul_nt(a, wt, layer, row0, n, out_dtype, name):
    m, k = a.shape
    tm, tn = _tile(m, 2048), _tile(n, 512)
    assert row0 % 8 == 0
    return pl.pallas_call(
        _mm_nt_kernel,
        out_shape=jax.ShapeDtypeStruct((m, n), out_dtype),
        grid=(m // tm, n // tn),
        in_specs=_lhs_slabs(tm, k) + [pl.BlockSpec((pl.Element(1), pl.Element(tn), pl.Element(k)),
                                                   lambda i, j: (layer, pl.multiple_of(row0 + j * tn, 8), 0))],
        out_specs=pl.BlockSpec((tm, tn), lambda i, j: (i, j)),
        compiler_params=_cparams("arbitrary", "arbitrary"), name=name,
    )(*([a] * LHS_SPLIT), wt)


def _mm_acc_kernel(a_ref, w_ref, o_ref):
    @pl.when(pl.program_id(2) == 0)
    def _():
        o_ref[...] = jnp.zeros(o_ref.shape, F32)

    o_ref[...] += _dot(a_ref[...], w_ref[...])


def _matmul_kacc(a, w, layer, name):
    m, k = a.shape
    n = w.shape[2]
    tm, tn, tk = _tile(m, 1024), _tile(n, 2048), _tile(k, 2048)
    return pl.pallas_call(
        _mm_acc_kernel,
        out_shape=jax.ShapeDtypeStruct((m, n), F32),
        grid=(m // tm, n // tn, k // tk),
        in_specs=[pl.BlockSpec((tm, tk), lambda i, j, kk: (i, kk)),
                  pl.BlockSpec((None, tk, tn), lambda i, j, kk: (layer, kk, j))],
        out_specs=pl.BlockSpec((tm, tn), lambda i, j, kk: (i, j)),
        compiler_params=_cparams("arbitrary", "arbitrary", "arbitrary"), name=name,
    )(a, w)


def _kiwi_kernel(a_ref, wt_ref, ci_ref, si_ref, o_ref, ob_ref):
    x = _dot_nt(a_ref[...], wt_ref[0].astype(BF16))
    lane = _iota(x.shape, 1)
    x = jnp.where(lane < IDX_DIM + IDX_HEADS, x, 0.0)
    is_k = lane < IDX_DIM
    c = jnp.where(is_k, ci_ref[...], 1.0)
    s = jnp.where(is_k, si_ref[...], 0.0)
    y = _rope(x, c, s, IDX_ROT_DIM // 2, IDX_DIM)
    wscale = (IDX_HEADS ** -0.5) * (IDX_DIM ** -0.5)
    y = y * jnp.where(is_k, 1.0, wscale)
    o_ref[...] = y
    ob_ref[...] = y.astype(BF16)


def _kiwi(h, wt, layer, row0, ci, si):
    n, k = h.shape
    tm = _tile(n, 1024)
    assert row0 % 8 == 0 and row0 + V7X_LANES <= wt.shape[1]
    row = pl.BlockSpec((tm, V7X_LANES), lambda i: (i, 0))
    return pl.pallas_call(
        _kiwi_kernel,
        out_shape=(jax.ShapeDtypeStruct((n, V7X_LANES), F32), jax.ShapeDtypeStruct((n, V7X_LANES), BF16)),
        grid=(n // tm,),
        in_specs=[pl.BlockSpec((tm, k), lambda i: (i, 0)),
                  pl.BlockSpec((pl.Element(1), pl.Element(V7X_LANES), pl.Element(k)),
                               lambda i: (layer, row0, 0)), row, row],
        out_specs=(row, row),
        compiler_params=_cparams("arbitrary"), name="inproj_indexer_kw",
    )(h, wt, ci, si)


def _prep_kernel(va_ref, qc_ref, kc_ref, vc_ref, qi_ref, ca_ref, sa_ref, ci_ref, si_ref,
                 vat_ref, vct_ref, qcr_ref, kcr_ref, qiz_ref):
    heads = va_ref.shape[1] // HEAD_DIM
    ca, sa, ci, si = ca_ref[...], sa_ref[...], ci_ref[...], si_ref[...]
    for h in range(heads):
        sl = slice(h * HEAD_DIM, (h + 1) * HEAD_DIM)
        vat_ref[h] = va_ref[:, sl].astype(F32).T.astype(BF16)
        vct_ref[h] = vc_ref[:, sl].astype(F32).T.astype(BF16)
        qcr_ref[:, sl] = _rope(qc_ref[:, sl].astype(F32), ca, sa, ROT_DIM // 2, HEAD_DIM).astype(BF16)
        kcr_ref[:, sl] = _rope(kc_ref[:, sl].astype(F32), ca, sa, ROT_DIM // 2, HEAD_DIM).astype(BF16)
    lane = _iota(ci.shape, 1)
    for t in range(IDX_HEADS * IDX_DIM // V7X_LANES):
        sl = slice(t * V7X_LANES, (t + 1) * V7X_LANES)
        y = _rope(qi_ref[:, sl].astype(F32), ci, si, IDX_ROT_DIM // 2, IDX_DIM)
        even = jnp.where(lane < IDX_DIM, y, 0.0)
        odd = jnp.where(lane < IDX_DIM, pltpu.roll(y, IDX_DIM, 1), 0.0)
        qiz_ref[:, 2 * t * V7X_LANES:(2 * t + 1) * V7X_LANES] = even.astype(BF16)
        qiz_ref[:, (2 * t + 1) * V7X_LANES:(2 * t + 2) * V7X_LANES] = odd.astype(BF16)


def _prep(proj, tabs, lay, batch, seq):
    n = proj.shape[0]
    bw, heads = lay["bw"], lay["bw"] // HEAD_DIM
    qw = IDX_HEADS * IDX_DIM
    tm = _tile(seq, 512)
    tpb = seq // tm

    def cols(off, width):
        return pl.BlockSpec((tm, width), lambda i: (i, off // width))

    tab = pl.BlockSpec((tm, V7X_LANES), lambda i: (i, 0))
    vt = pl.BlockSpec((None, heads, HEAD_DIM, tm), lambda i: (i // tpb, 0, 0, i % tpb))
    rowb = pl.BlockSpec((tm, bw), lambda i: (i, 0))
    ca, sa, ci, si = tabs
    return pl.pallas_call(
        _prep_kernel,
        out_shape=(jax.ShapeDtypeStruct((batch, heads, HEAD_DIM, seq), BF16),
                   jax.ShapeDtypeStruct((batch, heads, HEAD_DIM, seq), BF16),
                   jax.ShapeDtypeStruct((n, bw), BF16),
                   jax.ShapeDtypeStruct((n, bw), BF16),
                   jax.ShapeDtypeStruct((n, 2 * qw), BF16)),
        grid=(n // tm,),
        in_specs=[cols(lay["va"], bw), cols(lay["qc"], bw), cols(lay["kc"], bw), cols(lay["vc"], bw),
                  cols(lay["qi"], qw), tab, tab, tab, tab],
        out_specs=(vt, vt, rowb, rowb, pl.BlockSpec((tm, 2 * qw), lambda i: (i, 0))),
        compiler_params=_cparams("arbitrary"), name="attn_prep",
    )(proj, proj, proj, proj, proj, ca, sa, ci, si)


def _sb_kernel(q_ref, k_ref, vt_ref, o_ref, acc_ref, *, tq, scale):
    i = pl.program_id(1)
    heads = q_ref.shape[1] // HEAD_DIM
    row, col = _iota((tq, tq), 0), _iota((tq, tq), 1)
    after = jnp.where(col > row, 1.0, 0.0).astype(BF16)
    causal = row < col

    def all_heads(s0, carry, diag):
        hr = range(heads)
        hs = [slice(h * HEAD_DIM, (h + 1) * HEAD_DIM) for h in hr]
        zs = [_dot_nt(k_ref[pl.ds(s0, tq), hs[h]], q_ref[:, hs[h]]) for h in hr]
        lss, l1ms, his, los = [], [], [], []
        for h in hr:
            z = zs[h] * scale
            ls = jnp.minimum(z, 0.0) - jnp.log(1.0 + jnp.exp(-jnp.abs(z)))
            l1m = ls - z
            if diag:
                l1m = jnp.where(causal, l1m, 0.0)
            hi = l1m.astype(BF16)
            lss.append(ls)
            l1ms.append(l1m)
            his.append(hi)
            los.append((l1m - hi.astype(F32)).astype(BF16))
        excls = [_dot(after, his[h]) + _dot(after, los[h]) for h in hr]
        ps = []
        for h in hr:
            a = jnp.exp(lss[h] + excls[h] + carry[h:h + 1])
            if diag:
                a = jnp.where(causal, a, 0.0)
            ps.append(a.astype(BF16))
        pvs = [_dot(vt_ref[h, :, pl.ds(s0, tq)], ps[h]) for h in hr]
        for h in hr:
            acc_ref[h] = pvs[h] if diag else acc_ref[h] + pvs[h]
        return carry + jnp.concatenate([jnp.sum(l1ms[h], axis=0, keepdims=True) for h in hr], axis=0)

    def alive(carry):
        return (jnp.max(carry) > SB_DEAD).astype(I32)

    carry = all_heads(pl.multiple_of(i * tq, tq), jnp.zeros((heads, tq), F32), True)

    def body(st):
        j, _, carry = st
        carry = all_heads(pl.multiple_of((i - 1 - j) * tq, tq), carry, False)
        return j + 1, alive(carry), carry

    lax.while_loop(lambda st: (st[0] < i) & (st[1] > 0), body, (jnp.int32(0), alive(carry), carry))
    for h in range(heads):
        o_ref[:, h * HEAD_DIM:(h + 1) * HEAD_DIM] = acc_ref[h].T.astype(o_ref.dtype)


def _sb_attention(proj, vat, lay, batch, seq):
    n = proj.shape[0]
    bw, heads = lay["bw"], lay["bw"] // HEAD_DIM
    tq = _tile(seq, 256)
    nq = seq // tq
    qb, kb = lay["qa"] // bw, lay["ka"] // bw
    return pl.pallas_call(
        functools.partial(_sb_kernel, tq=tq, scale=HEAD_DIM ** -0.5),
        out_shape=jax.ShapeDtypeStruct((n, bw), BF16),
        grid=(batch, nq),
        in_specs=[pl.BlockSpec((tq, bw), lambda b, i: (b * nq + i, qb)),
                  pl.BlockSpec((seq, bw), lambda b, i: (b, kb)),
                  pl.BlockSpec((None, heads, HEAD_DIM, seq), lambda b, i: (b, 0, 0, 0))],
        out_specs=pl.BlockSpec((tq, bw), lambda b, i: (b * nq + i, 0)),
        scratch_shapes=[pltpu.VMEM((heads, HEAD_DIM, tq), F32)],
        compiler_params=_cparams("arbitrary", "arbitrary"), name="sb_attention",
    )(proj, proj, vat)


def _conformer_kernel(a_ref, g_ref, ha_ref, hg_ref, cw_ref, cb_ref, lg_ref, lb_ref, o_ref,
                      u_ref, y_ref, *, tm, tpb, rc, cc):
    width, ch = cw_ref.shape
    first = (pl.program_id(0) % tpb) == 0
    hu = ha_ref[...].astype(F32) * jax.nn.sigmoid(hg_ref[...].astype(F32))
    u_ref[0:CONV_HALO, :] = jnp.where(first, 0.0, hu)
    u_ref[CONV_HALO:, :] = a_ref[...].astype(F32) * jax.nn.sigmoid(g_ref[...].astype(F32))
    lead = CONV_HALO - (width - 1)

    def rows(r, _):
        r0 = pl.multiple_of(r * rc, rc)
        for c in range(ch // cc):
            cs = slice(c * cc, (c + 1) * cc)
            win = u_ref[pl.ds(r0, rc + CONV_HALO), cs]
            acc = jnp.zeros((rc, cc), F32)
            for k in range(width):
                acc = acc + cw_ref[k:k + 1, cs] * win[lead + k:lead + k + rc, :]
            y_ref[:, cs] = acc + cb_ref[:, cs]
        y = y_ref[...]
        mu = jnp.mean(y, axis=-1, keepdims=True)
        yc = y - mu
        var = jnp.mean(yc * yc, axis=-1, keepdims=True)
        yn = yc * lax.rsqrt(var + EPS) * lg_ref[...] + lb_ref[...]
        o_ref[pl.ds(r0, rc), :] = (yn * jax.nn.sigmoid(yn)).astype(o_ref.dtype)
        return 0

    lax.fori_loop(0, tm // rc, rows, 0)


def _conformer(proj, conv_w, conv_b, ln_g, ln_b, lay, seq):
    n = proj.shape[0]
    width, ch = conv_w.shape
    assert width - 1 <= CONV_HALO
    tm = _tile(seq, 512)
    tpb = seq // tm
    rc, cc = _tile(tm, 64), _tile(ch, 256)
    a_blk, hb = lay["glu"] // ch, tm // CONV_HALO
    vec = pl.BlockSpec((1, ch), lambda i: (0, 0))
    return pl.pallas_call(
        functools.partial(_conformer_kernel, tm=tm, tpb=tpb, rc=rc, cc=cc),
        out_shape=jax.ShapeDtypeStruct((n, ch), BF16),
        grid=(n // tm,),
        in_specs=[pl.BlockSpec((tm, ch), lambda i: (i, a_blk)),
                  pl.BlockSpec((tm, ch), lambda i: (i, a_blk + 1)),
                  pl.BlockSpec((CONV_HALO, ch), lambda i: (jnp.maximum(i * hb - 1, 0), a_blk)),
                  pl.BlockSpec((CONV_HALO, ch), lambda i: (jnp.maximum(i * hb - 1, 0), a_blk + 1)),
                  pl.BlockSpec((width, ch), lambda i: (0, 0)), vec, vec, vec],
        out_specs=pl.BlockSpec((tm, ch), lambda i: (i, 0)),
        scratch_shapes=[pltpu.VMEM((tm + CONV_HALO, ch), F32), pltpu.VMEM((rc, ch), F32)],
        compiler_params=_cparams("arbitrary"), name="conformer_conv",
    )(proj, proj, proj, proj, conv_w, conv_b.reshape(1, ch), ln_g.reshape(1, ch), ln_b.reshape(1, ch))


def _dsa_kernel(qc_ref, kc_ref, vt_ref, qiz_ref, kb_ref, kwq_ref, o_ref, key_ref, acc_ref, *, tq, k_top, scale):
    i = pl.program_id(1)
    heads = qc_ref.shape[1] // HEAD_DIM
    seq_bits = max(1, (kc_ref.shape[0] - 1).bit_length())
    row, col = _iota((tq, tq), 0), _iota((tq, tq), 1)
    admissible = row <= col
    wi_t = kwq_ref[...].T

    def start(c):
        return pl.multiple_of(c * tq, tq)

    def score_tile(c, diag):
        kb = kb_ref[pl.ds(start(c), tq), :]
        acc = jnp.zeros((tq, tq), F32)
        for h in range(IDX_HEADS):
            d = _dot_nt(kb, qiz_ref[:, h * V7X_LANES:(h + 1) * V7X_LANES])
            acc = acc + jnp.maximum(d, 0.0) * wi_t[IDX_DIM + h:IDX_DIM + h + 1, :]
        if diag:
            acc = jnp.where(admissible, acc, -jnp.inf)
        bits = lax.bitcast_convert_type(acc, I32)
        key_ref[pl.ds(start(c), tq), :] = bits ^ (lax.shift_right_arithmetic(bits, 31) & 0x7FFFFFFF)

    def score_body(c, _):
        score_tile(c, False)
        return 0

    lax.fori_loop(0, i, score_body, 0)
    score_tile(i, True)

    def count(pred):
        def body(c, cnt):
            hit = pred(key_ref[pl.ds(start(c), tq), :], c)
            ones = jnp.where(hit, 1.0, 0.0).reshape(tq // 8, 8, tq)
            return cnt + jnp.sum(ones, axis=0)

        def pair(cc, cnt):
            return body(2 * cc + 1, body(2 * cc, cnt))

        pairs = (i + 1) // 2
        part = lax.fori_loop(0, pairs, pair, jnp.zeros((8, tq), F32))
        part = lax.fori_loop(2 * pairs, i + 1, body, part)
        return jnp.sum(part, axis=0, keepdims=True)

    def unsettled(n_sel):
        return (jnp.max(jnp.abs(n_sel - k_top)) > 0).astype(I32)

    def bit_body(st):
        b, _, ans, n_sel = st
        cand = ans | lax.shift_left(jnp.int32(1), 31 - b)
        cnt = count(lambda k, c: k >= (cand ^ INT_MIN))
        take = cnt >= k_top
        n_sel = jnp.where(take, cnt, n_sel)
        return b + 1, unsettled(n_sel), jnp.where(take, cand, ans), n_sel

    n_all = jnp.zeros((1, tq), F32) + ((i + 1) * tq).astype(F32)
    _, _, ans, n_ge = lax.while_loop(lambda st: (st[0] < 32) & (st[1] > 0), bit_body,
                                     (jnp.int32(0), unsettled(n_all), jnp.zeros((1, tq), I32), n_all))
    thr = ans ^ INT_MIN


    @pl.when(jnp.max(n_ge) > k_top)
    def _():
        need = k_top - count(lambda k, c: k > thr)

        def pos_body(b, x):
            cand = x | lax.shift_left(jnp.int32(1), seq_bits - 1 - b)
            below = count(lambda k, c: (k == thr) & (row + c * tq < cand))
            return jnp.where(below < need, cand, x)

        last = lax.fori_loop(0, seq_bits, pos_body, jnp.zeros((1, tq), I32))

        def demote(c, _):
            k = key_ref[pl.ds(start(c), tq), :]
            drop = (k == thr) & (row + c * tq > last) & (n_ge > k_top)
            key_ref[pl.ds(start(c), tq), :] = jnp.where(drop, thr - 1, k)
            return 0

        lax.fori_loop(0, i + 1, demote, 0)

    def attend(c, st, diag):
        m, l = st
        s0 = start(c)
        sel = key_ref[pl.ds(s0, tq), :] >= thr
        if diag:
            sel = sel & admissible
        ms, ls = [], []
        for g0 in range(0, heads, HEAD_GROUP):
            attend_group(range(g0, min(g0 + HEAD_GROUP, heads)), s0, sel, m, l, ms, ls)
        return jnp.concatenate(ms, axis=0), jnp.concatenate(ls, axis=0)

    def attend_group(hr, s0, sel, m, l, ms, ls):
        hs = {h: slice(h * HEAD_DIM, (h + 1) * HEAD_DIM) for h in hr}
        lgs = {h: _dot_nt(kc_ref[pl.ds(s0, tq), hs[h]], qc_ref[:, hs[h]]) for h in hr}
        alphas, ps = {}, {}
        for h in hr:
            lgm = jnp.where(sel, lgs[h], MASKED)
            m_new = jnp.maximum(m[h:h + 1], jnp.max(lgm, axis=0, keepdims=True))
            alpha = jnp.exp2((m[h:h + 1] - m_new) * (scale * LOG2E))
            p = jnp.where(sel, jnp.exp2((lgm - m_new) * (scale * LOG2E)), 0.0)
            ls.append(alpha * l[h:h + 1] + jnp.sum(p, axis=0, keepdims=True))
            ms.append(m_new)
            alphas[h] = alpha
            ps[h] = p.astype(BF16)
        pvs = {h: _dot(vt_ref[h, :, pl.ds(s0, tq)], ps[h]) for h in hr}
        for h in hr:
            acc_ref[h] = alphas[h] * acc_ref[h] + pvs[h]

    acc_ref[...] = jnp.zeros(acc_ref.shape, F32)
    st = (jnp.full((heads, tq), MASKED, F32), jnp.zeros((heads, tq), F32))
    st = lax.fori_loop(0, i, lambda c, s: attend(c, s, False), st)
    _, l = attend(i, st, True)
    for h in range(heads):
        o_ref[:, h * HEAD_DIM:(h + 1) * HEAD_DIM] = (acc_ref[h] / l[h:h + 1]).T.astype(o_ref.dtype)


def _dsa_attention(qcr, kcr, vct, qiz, kwb, kw, batch, seq, k_top):
    n, bw = qcr.shape
    heads = bw // HEAD_DIM
    tq = _tile(seq, 256)
    nq = seq // tq
    return pl.pallas_call(
        functools.partial(_dsa_kernel, tq=tq, k_top=k_top, scale=HEAD_DIM ** -0.5),
        out_shape=jax.ShapeDtypeStruct((n, bw), BF16),
        grid=(batch, nq),
        in_specs=[pl.BlockSpec((tq, bw), lambda b, i: (b * nq + i, 0)),
                  pl.BlockSpec((seq, bw), lambda b, i: (b, 0)),
                  pl.BlockSpec((None, heads, HEAD_DIM, seq), lambda b, i: (b, 0, 0, 0)),
                  pl.BlockSpec((tq, qiz.shape[1]), lambda b, i: (b * nq + i, 0)),
                  pl.BlockSpec((seq, V7X_LANES), lambda b, i: (b, 0)),
                  pl.BlockSpec((tq, V7X_LANES), lambda b, i: (b * nq + i, 0))],
        out_specs=pl.BlockSpec((tq, bw), lambda b, i: (b * nq + i, 0)),
        scratch_shapes=[pltpu.VMEM((seq, tq), I32), pltpu.VMEM((heads, HEAD_DIM, tq), F32)],
        compiler_params=_cparams("arbitrary", "arbitrary"), name="dsa_attention",
    )(qcr, kcr, vct, qiz, kwb, kw)


def _merge_kernel(a_ref, b_ref, c_ref, w_ref, ga_ref, gb_ref, gc_ref, o_ref, *, chunks):
    rc = o_ref.shape[0] // chunks
    w = [w_ref[g].astype(BF16) for g in range(3)]
    for r in range(chunks):
        rs = slice(r * rc, (r + 1) * rc)
        acc = jax.nn.sigmoid(ga_ref[rs, :].astype(F32)) * _dot(a_ref[rs, :], w[0])
        acc = acc + jax.nn.sigmoid(gb_ref[rs, :].astype(F32)) * _dot(b_ref[rs, :], w[1])
        acc = acc + jax.nn.sigmoid(gc_ref[rs, :].astype(F32)) * _dot(c_ref[rs, :], w[2])
        o_ref[rs, :] = acc.astype(o_ref.dtype)


def _merge(out_a, out_b, out_c, w_branch, layer, gates):
    n, bw = out_a.shape
    d = w_branch.shape[3]
    tm, tn = _tile(n, 2048), _tile(d, 512)
    gstep = d // tn
    br = _resident_rows(tm, bw)

    def gate(g):
        return pl.BlockSpec((tm, tn), lambda i, j: (i, g * gstep + j))

    return pl.pallas_call(
        functools.partial(_merge_kernel, chunks=4),
        out_shape=jax.ShapeDtypeStruct((n, d), BF16),
        grid=(n // tm, d // tn),
        in_specs=[br, br, br, pl.BlockSpec((None, 3, bw, tn), lambda i, j: (layer, 0, 0, j)),
                  gate(0), gate(1), gate(2)],
        out_specs=pl.BlockSpec((tm, tn), lambda i, j: (i, j)),
        compiler_params=_cparams("arbitrary", "arbitrary"), name="branch_merge",
    )(out_a, out_b, out_c, w_branch, gates, gates, gates)


def _ffn_up_kernel(*refs, tpb, chunks):
    a_refs = refs[:LHS_SPLIT]
    wg_ref, wv_ref, cg_ref, cv_ref, o_ref, carry_ref = refs[LHS_SPLIT:]
    i, j = pl.program_id(0), pl.program_id(1)
    tm, tn = o_ref.shape
    rc = tm // chunks
    kc = a_refs[0].shape[1]
    width = cg_ref.shape[0]
    w = jnp.concatenate([wg_ref[...].astype(BF16), wv_ref[...].astype(BF16)], axis=1)
    cw = jnp.concatenate([cg_ref[...], cv_ref[...]], axis=1)

    @pl.when(i % tpb == 0)
    def _():
        carry_ref[j] = jnp.zeros(carry_ref.shape[1:], F32)

    tail = carry_ref[j]
    for r in range(chunks):
        p = None
        for c, a_ref in enumerate(a_refs):
            part = _dot(a_ref[r * rc:(r + 1) * rc, :], w[c * kc:(c + 1) * kc, :])
            p = part if p is None else p + part
        ext = jnp.concatenate([tail, p], axis=0)
        out = cw[width - 1:width, :] * p
        for k in range(width - 1):
            sh = width - 1 - k
            out = out + cw[k:k + 1, :] * ext[8 - sh:8 - sh + rc, :]
        tail = p[rc - 8:, :]
        g, v = out[:, :tn], out[:, tn:]
        o_ref[r * rc:(r + 1) * rc, :] = (g * jax.nn.sigmoid(g) * v).astype(o_ref.dtype)
    carry_ref[j] = tail


def _ffn_up(h, w_up, layer, conv_w, seq):
    n, d = h.shape
    width, f2 = conv_w.shape
    f = f2 // 2
    assert width - 1 <= 8
    tm, tn = _tile(seq, 2048), _tile(f, 256)
    tpb, nj = seq // tm, f // tn
    return pl.pallas_call(
        functools.partial(_ffn_up_kernel, tpb=tpb, chunks=8),
        out_shape=jax.ShapeDtypeStruct((n, f), BF16),
        grid=(n // tm, nj),
        in_specs=_lhs_slabs(tm, d) + [
            pl.BlockSpec((None, d, tn), lambda i, j: (layer, 0, j)),
            pl.BlockSpec((None, d, tn), lambda i, j: (layer, 0, nj + j)),
            pl.BlockSpec((width, tn), lambda i, j: (0, j)),
            pl.BlockSpec((width, tn), lambda i, j: (0, nj + j))],
        out_specs=pl.BlockSpec((tm, tn), lambda i, j: (i, j)),
        scratch_shapes=[pltpu.VMEM((nj, 8, 2 * tn), F32)],
        compiler_params=_cparams("arbitrary", "arbitrary"), name="ffn_up_conv_gate",
    )(*([h] * LHS_SPLIT), w_up, w_up, conv_w, conv_w)


def _layout(bw, ch):
    lay = {"bw": bw, "qa": 0, "ka": bw, "va": 2 * bw, "glu": 3 * bw}
    lay["qc"] = 3 * bw + 2 * ch
    lay["kc"] = lay["qc"] + bw
    lay["vc"] = lay["kc"] + bw
    lay["qi"] = lay["vc"] + bw
    lay["gl"] = lay["qi"] + IDX_HEADS * IDX_DIM
    return lay


def kernel(x, c, positions, ada_w, ada_b, norm_pre_mix, norm_post_mix, norm_pre_ffn, norm_post_ffn,
           w_in, conv_w, conv_b, conv_ln_g, conv_ln_b, w_branch, w_out, w_up, ffn_conv_w, w_down):
    batch, seq, d = x.shape
    depth = ada_w.shape[0]
    bw, ch = w_branch.shape[2], conv_w.shape[2]
    assert bw == ch and bw % HEAD_DIM == 0 and batch <= 8
    n = batch * seq
    k_top = min(TOPK_MAX, seq // 4)
    lay = _layout(bw, ch)
    small0 = lay["gl"]
    small_w = IDX_DIM + IDX_HEADS

    tabs = _rope_tables(positions)
    c8 = jnp.zeros((8, d), F32).at[:batch].set(c)
    mod = _modulation(c8, ada_w, ada_b)
    mod = mod[:, :batch].reshape(depth, batch, N_MOD, d).transpose(0, 2, 1, 3)
    mod = mod.reshape(depth, N_MOD, batch, 1, d)
    SH_M, SC_M, G_M, SH_F, SC_F, G_F = range(N_MOD)

    w_in_t = jnp.swapaxes(w_in, 1, 2)
    w_down_b = w_down.astype(BF16)

    xf = x.reshape(n, d)
    h = _prenorm(xf, norm_pre_mix[0], mod[0], SC_M, SH_M, seq)
    for l in range(depth):
        proj = _matmul_nt(h, w_in_t, l, 0, small0, BF16, "inproj_main")
        gates = _matmul_nt(h, w_in_t, l, small0 + small_w, w_branch.shape[1] * d, BF16, "inproj_gates")
        kw, kwb = _kiwi(h, w_in_t, l, small0, tabs[2], tabs[3])
        vat, vct, qcr, kcr, qiz = _prep(proj, tabs, lay, batch, seq)
        out_a = _sb_attention(proj, vat, lay, batch, seq)
        out_b = _conformer(proj, conv_w[l], conv_b[l], conv_ln_g[l], conv_ln_b[l], lay, seq)
        out_c = _dsa_attention(qcr, kcr, vct, qiz, kwb, kw, batch, seq, k_top)
        merged = _merge(out_a, out_b, out_c, w_branch, l, gates)
        y = _matmul(merged, w_out, l, d, F32, "outproj")
        xf, h = _post(y, xf, norm_post_mix[l], mod[l], G_M, seq,
                      nxt=(norm_pre_ffn[l], mod[l], SC_F, SH_F))

        act = _ffn_up(h, w_up, l, ffn_conv_w[l], seq)
        y = _matmul_kacc(act, w_down_b, l, "ffn_down")
        nxt = (norm_pre_mix[l + 1], mod[l + 1], SC_M, SH_M) if l + 1 < depth else None
        xf, h = _post(y, xf, norm_post_ffn[l], mod[l], G_F, seq, nxt=nxt)
    return xf.reshape(batch, seq, d)
```
